```python
import functools
import jax, jax.numpy as jnp
from jax import lax
import numpy as np

D_MODEL = 1024
BATCH = 8
SEQ = 2048
DEPTH = 4
DEC_BATCH = 128
DEC_SEQ = 4
PAST_LEN = 2048
PAGE_SIZE = 128

MIX_WIDTH = D_MODEL
W_A = MIX_WIDTH // 4
W_B = MIX_WIDTH // 4
W_C = MIX_WIDTH // 4
W_D = MIX_WIDTH - W_A - W_B - W_C

A_HEADS = 4
A_HEAD_DIM = W_A // A_HEADS
ROT_DIM = A_HEAD_DIM // 4
ROPE_THETA = 500000.0
IDX_HEADS = 4
IDX_DIM = 32
IDX_ROT_DIM = IDX_DIM // 4
TOPK_MAX = 256
Q_BLOCK = 128

B_HEAD = 64
B_HEADS = W_B // B_HEAD
DECAY_LORA = 32
AAA_LORA = 32
GATE_LORA = 64
B_PROJ = 3 * W_B + DECAY_LORA + AAA_LORA + GATE_LORA
LNX_EPS = 64e-5

C_CONV = 3

D_HEADS = 4
D_BLOCK = W_D // D_HEADS
D_CONV = 4
LRU_C = 8.0

N_EXPERTS = 16
N_GROUPS = 4
EXPERTS_PER_GROUP = N_EXPERTS // N_GROUPS
TOP_K_EXPERTS = 2
D_EXPERT = 256

LN_EPS = 1e-5
DEEPNORM_ALPHA = (2 * DEPTH) ** 0.25
DEEPNORM_BETA = (8 * DEPTH) ** -0.25

A_Q = 0
A_K = A_Q + W_A
A_V = A_K + W_A
A_QI = A_V + W_A
A_KI = A_QI + IDX_HEADS * IDX_DIM
A_WI = A_KI + IDX_DIM
A_END = A_WI + IDX_HEADS
B_START = A_END
B_END = B_START + B_PROJ
C_START = B_END
C_END = C_START + 3 * W_C
D_START = C_END
D_END = D_START + 2 * W_D
IN_COLS = D_END

kernel_name = 'hymba_dsa_rwkv7_conv_rglru_moe_step'


def layer_norm(x, g, b):
    xf = x.astype(jnp.float32)
    mu = jnp.mean(xf, -1, keepdims=True)
    var = jnp.mean(jnp.square(xf - mu), -1, keepdims=True)
    return ((xf - mu) * lax.rsqrt(var + LN_EPS) * g + b).astype(x.dtype)


def rope_partial(x, pos, rot_dim):
    half = rot_dim // 2
    inv_freq = ROPE_THETA ** (-jnp.arange(half, dtype=jnp.float32) / half)
    ang = pos.astype(jnp.float32)[:, None] * inv_freq[None, :]
    cos = jnp.cos(ang)[:, None, :].astype(x.dtype)
    sin = jnp.sin(ang)[:, None, :].astype(x.dtype)
    x1, x2, rest = x[..., :half], x[..., half:rot_dim], x[..., rot_dim:]
    return jnp.concatenate([x1 * cos - x2 * sin, x2 * cos + x1 * sin, rest], axis=-1)


def causal_dwconv(x, buf, w):
    width = w.shape[0]
    t = x.shape[1]
    xx = jnp.concatenate([buf.astype(x.dtype), x], axis=1)
    out = w[0] * xx[:, 0:t]
    for j in range(1, width):
        out = out + w[j] * xx[:, j:j + t]
    return out, xx[:, t:]


def gather_pages(pool, layer, page_table):
    g = pool[layer, page_table]
    return g.reshape((page_table.shape[0], -1) + pool.shape[3:])


def dsa_attend(q, qi, wi, k, v, ki, q_pos, topk):
    f32 = jnp.float32
    n_keys = k.shape[1]
    dots = jnp.einsum('bthd,bsd->bths', qi.astype(f32), ki.astype(f32)) * IDX_DIM ** -0.5
    score = jnp.einsum('bth,bths->bts', wi.astype(f32), jax.nn.relu(dots))
    admissible = jnp.arange(n_keys)[None, :] <= q_pos[:, None]
    score = jnp.where(admissible[None], score, -jnp.inf)
    _, sel = lax.top_k(score, topk)
    valid = sel <= q_pos[None, :, None]
    take = jax.vmap(lambda arr, ix: arr[ix])
    k_sel = take(k, sel)
    v_sel = take(v, sel)
    logits = jnp.einsum('bthd,btkhd->bthk', q.astype(f32), k_sel.astype(f32)) * A_HEAD_DIM ** -0.5
    logits = jnp.where(valid[:, :, None, :], logits, -jnp.inf)
    p = jax.nn.softmax(logits, axis=-1)
    return jnp.einsum('bthk,btkhd->bthd', p.astype(v.dtype), v_sel)


def dsa_prompt(q, k, v, qi, ki, wi):
    b, t = q.shape[0], q.shape[1]
    nb = t // Q_BLOCK
    topk = min(TOPK_MAX, t // 4)

    def blockify(a):
        return jnp.swapaxes(a.reshape((b, nb, Q_BLOCK) + a.shape[2:]), 0, 1)

    pos = jnp.arange(t).reshape(nb, Q_BLOCK)
    out = lax.map(lambda a: dsa_attend(a[0], a[1], a[2], k, v, ki, a[3], topk),
                  (blockify(q), blockify(qi), blockify(wi), pos))
    return jnp.swapaxes(out, 0, 1).reshape(b, t, A_HEADS, A_HEAD_DIM)


def dsa_sample(q, k, v, qi, ki, wi, past):
    past_k, past_v, past_ki = past
    k_all = jnp.concatenate([past_k.astype(k.dtype), k], axis=1)
    v_all = jnp.concatenate([past_v.astype(v.dtype), v], axis=1)
    ki_all = jnp.concatenate([past_ki.astype(ki.dtype), ki], axis=1)
    n_keys = k_all.shape[1]
    q_pos = past_k.shape[1] + jnp.arange(q.shape[1])
    return dsa_attend(q, qi, wi, k_all, v_all, ki_all, q_pos, min(TOPK_MAX, n_keys // 4))


def rwkv7_scan(r, w, k, v, a_vec, b_vec, state):
    def step(s, inp):
        r_t, w_t, k_t, v_t, a_t, b_t = inp
        sa = jnp.einsum('bhvk,bhk->bhv', s, a_t)
        s = s * w_t[:, :, None, :] + sa[..., None] * b_t[:, :, None, :] + v_t[..., None] * k_t[:, :, None, :]
        return s, jnp.einsum('bhvk,bhk->bhv', s, r_t)
    xs = tuple(jnp.moveaxis(z, 1, 0) for z in (r, w, k, v, a_vec, b_vec))
    s, ys = lax.scan(step, state, xs)
    return jnp.moveaxis(ys, 0, 1), s


def rwkv7_mix(p, shift_buf, state, lp):
    f32 = jnp.float32
    b, t = p.shape[0], p.shape[1]
    prev = jnp.concatenate([shift_buf.astype(p.dtype), p[:, :-1]], axis=1)
    z = p + (prev - p) * lp['rwkv_mu']
    r, k, v, wl, al, gl = jnp.split(z, [W_B, 2 * W_B, 3 * W_B, 3 * W_B + DECAY_LORA,
                                        3 * W_B + DECAY_LORA + AAA_LORA], axis=-1)
    w_log = -jax.nn.softplus(-(lp['rwkv_w0'] + jnp.tanh(wl) @ lp['rwkv_w2']).astype(f32)) - 0.5
    decay = jnp.exp(-jnp.exp(w_log))
    a = jax.nn.sigmoid((lp['rwkv_a0'] + al @ lp['rwkv_a2']).astype(f32))
    g = (jax.nn.sigmoid(gl) @ lp['rwkv_g2']).astype(f32)

    def heads(u):
        return u.astype(f32).reshape(b, t, B_HEADS, B_HEAD)

    kk = heads(k * lp['rwkv_kk'])
    kk = kk / jnp.maximum(jnp.sqrt(jnp.sum(kk * kk, -1, keepdims=True)), 1e-12)
    kf = k.astype(f32) * (1.0 + (a - 1.0) * lp['rwkv_ka'].astype(f32))
    rh, kh, vh, ah = heads(r), heads(kf), heads(v), heads(a)
    y, s_new = rwkv7_scan(rh, heads(decay), kh, vh, -kk, kk * ah, state.astype(f32))
    mu = jnp.mean(y, -1, keepdims=True)
    var = jnp.mean(jnp.square(y - mu), -1, keepdims=True)
    y = ((y - mu) * lax.rsqrt(var + LNX_EPS)).reshape(b, t, W_B) * lp['rwkv_lnx_g'] + lp['rwkv_lnx_b']
    bonus = jnp.sum(rh * kh * lp['rwkv_rk'].astype(f32), -1, keepdims=True) * vh
    y = (y + bonus.reshape(b, t, W_B)) * g
    return y.astype(p.dtype), s_new.astype(state.dtype), p[:, -1:]


def gated_short_conv(p, buf, conv_w):
    bg, cg, xin = jnp.split(p, 3, axis=-1)
    y, new_buf = causal_dwconv(cg * xin, buf, conv_w)
    return bg * y, new_buf


def lru_scan(a, u, h0):
    def step(h, inp):
        h = inp[0] * h + inp[1]
        return h, h
    h, hs = lax.scan(step, h0, (jnp.moveaxis(a, 1, 0), jnp.moveaxis(u, 1, 0)))
    return jnp.moveaxis(hs, 0, 1), h


def rglru_block(p, buf, h0, lp):
    f32 = jnp.float32
    b, t = p.shape[0], p.shape[1]
    xb, gb = jnp.split(p, 2, axis=-1)
    xc, new_buf = causal_dwconv(xb, buf, lp['convd_w'])
    xc = xc + lp['convd_b']
    xh = xc.reshape(b, t, D_HEADS, D_BLOCK)
    gate_x = jax.nn.sigmoid((jnp.einsum('bthi,hij->bthj', xh, lp['lru_gx_w']).reshape(b, t, W_D)
                             + lp['lru_gx_b']).astype(f32))
    gate_a = jax.nn.sigmoid((jnp.einsum('bthi,hij->bthj', xh, lp['lru_ga_w']).reshape(b, t, W_D)
                             + lp['lru_ga_b']).astype(f32))
    log_a = -LRU_C * gate_a * jax.nn.softplus(-lp['lru_lam'].astype(f32))
    u = jnp.sqrt(-jnp.expm1(2.0 * log_a)) * gate_x * xc.astype(f32)
    hs, h = lru_scan(jnp.exp(log_a), u, h0.astype(f32))
    y = hs * jax.nn.gelu(gb.astype(f32))
    return y.astype(p.dtype), new_buf, h.astype(h0.dtype)


def moe(x, w_router, router_bias, w_gate, w_up, w_down):
    f32 = jnp.float32
    shp = x.shape
    xf = x.reshape(-1, shp[-1])
    n = xf.shape[0]
    s = jax.nn.sigmoid((xf @ w_router).astype(f32))
    sel = (s + router_bias.astype(f32)).reshape(n, N_GROUPS, EXPERTS_PER_GROUP)
    group_score = jnp.sum(lax.top_k(sel, TOP_K_EXPERTS)[0], -1)
    g_idx = jnp.argmax(group_score, -1)
    in_grp = jnp.take_along_axis(sel, g_idx[:, None, None], axis=1)[:, 0]
    _, local = lax.top_k(in_grp, TOP_K_EXPERTS)
    e_idx = g_idx[:, None] * EXPERTS_PER_GROUP + local
    w_sel = jnp.take_along_axis(s, e_idx, axis=1)
    w_sel = w_sel / jnp.sum(w_sel, -1, keepdims=True)
    gates = jnp.sum(jax.nn.one_hot(e_idx, N_EXPERTS, dtype=f32) * w_sel[..., None], axis=1)
    h = jax.nn.silu(jnp.einsum('nd,edf->nef', xf, w_gate)) * jnp.einsum('nd,edf->nef', xf, w_up)
    y = jnp.einsum('nef,efd->nd', h * gates[:, :, None].astype(h.dtype), w_down)
    return y.reshape(shp)


def layer_step(x, pos, lp, attn, rwkv_state, shift_buf, convc_buf, convd_buf, lru_h, w_router, router_bias):
    b, t = x.shape[0], x.shape[1]
    cols = x @ lp['w_in']
    q = rope_partial(cols[..., A_Q:A_K].reshape(b, t, A_HEADS, A_HEAD_DIM), pos, ROT_DIM)
    k = rope_partial(cols[..., A_K:A_V].reshape(b, t, A_HEADS, A_HEAD_DIM), pos, ROT_DIM)
    v = cols[..., A_V:A_QI].reshape(b, t, A_HEADS, A_HEAD_DIM)
    qi = rope_partial(cols[..., A_QI:A_KI].reshape(b, t, IDX_HEADS, IDX_DIM), pos, IDX_ROT_DIM)
    ki = layer_norm(cols[..., A_KI:A_WI], lp['idx_ln_g'], lp['idx_ln_b'])
    ki = rope_partial(ki[:, :, None, :], pos, IDX_ROT_DIM)[:, :, 0]
    wi = cols[..., A_WI:A_END] * IDX_HEADS ** -0.5
    y_a = attn(q, k, v, qi, ki, wi).reshape(b, t, W_A)
    y_b, rwkv_state, shift_buf = rwkv7_mix(cols[..., B_START:B_END], shift_buf, rwkv_state, lp)
    y_c, convc_buf = gated_short_conv(cols[..., C_START:C_END], convc_buf, lp['convc_w'])
    y_d, convd_buf, lru_h = rglru_block(cols[..., D_START:D_END], convd_buf, lru_h, lp)
    mix = jnp.concatenate([y_a, y_b, y_c, y_d], axis=-1) @ lp['w_out']
    x = layer_norm(DEEPNORM_ALPHA * x + mix, lp['ln1_g'], lp['ln1_b'])
    ffn = moe(x, w_router, router_bias, lp['moe_w_gate'], lp['moe_w_up'], lp['moe_w_down'])
    x = layer_norm(DEEPNORM_ALPHA * x + ffn, lp['ln2_g'], lp['ln2_b'])
    return x, (k, v, ki, rwkv_state, shift_buf, convc_buf, convd_buf, lru_h)


def setup_inputs(seed: int = 0) -> dict:
    key = jax.random.key(seed)
    keys = iter(jax.random.split(key, 64))

    def nrm(shape, scale=1.0):
        return jax.random.normal(next(keys), shape, jnp.float32) * scale

    def unif(shape, lo, hi):
        return jax.random.uniform(next(keys), shape, jnp.float32, lo, hi)

    n_pages = PAST_LEN // PAGE_SIZE
    n_phys = (DEC_BATCH * n_pages * 5) // 4
    page_table = jax.random.permutation(next(keys), n_phys)[:DEC_BATCH * n_pages]
    page_table = page_table.reshape(DEC_BATCH, n_pages).astype(jnp.int32)
    lam_base = unif((DEPTH, W_D), 0.81, 0.998) ** (1.0 / LRU_C)
    lru_lam = jnp.log(lam_base) - jnp.log1p(-lam_base)
    return {
        'x_prompt': nrm((BATCH, SEQ, D_MODEL)),
        'x_sample': nrm((DEC_BATCH, DEC_SEQ, D_MODEL)),
        'cache_k': nrm((DEPTH, n_phys, PAGE_SIZE, A_HEADS, A_HEAD_DIM)),
        'cache_v': nrm((DEPTH, n_phys, PAGE_SIZE, A_HEADS, A_HEAD_DIM)),
        'cache_ik': nrm((DEPTH, n_phys, PAGE_SIZE, IDX_DIM)),
        'page_table': page_table,
        'state_rwkv': nrm((DEPTH, DEC_BATCH, B_HEADS, B_HEAD, B_HEAD), 0.3),
        'state_rwkv_shift': nrm((DEPTH, DEC_BATCH, 1, B_PROJ)),
        'state_conv_c': nrm((DEPTH, DEC_BATCH, C_CONV - 1, W_C)),
        'state_conv_d': nrm((DEPTH, DEC_BATCH, D_CONV - 1, W_D)),
        'state_lru': nrm((DEPTH, DEC_BATCH, W_D), 0.5),
        'w_in': nrm((DEPTH, D_MODEL, IN_COLS), D_MODEL ** -0.5),
        'w_out': nrm((DEPTH, MIX_WIDTH, D_MODEL), MIX_WIDTH ** -0.5 * DEEPNORM_BETA),
        'idx_ln_g': 1.0 + nrm((DEPTH, IDX_DIM), 0.01),
        'idx_ln_b': nrm((DEPTH, IDX_DIM), 0.01),
        'rwkv_mu': unif((DEPTH, B_PROJ), 0.0, 1.0),
        'rwkv_w0': unif((DEPTH, W_B), -4.0, 0.0),
        'rwkv_w2': nrm((DEPTH, DECAY_LORA, W_B), 0.1),
        'rwkv_a0': nrm((DEPTH, W_B), 0.1),
        'rwkv_a2': nrm((DEPTH, AAA_LORA, W_B), 0.1),
        'rwkv_g2': nrm((DEPTH, GATE_LORA, W_B), GATE_LORA ** -0.5),
        'rwkv_kk': 0.85 + nrm((DEPTH, W_B), 0.05),
        'rwkv_ka': 1.0 + nrm((DEPTH, W_B), 0.05),
        'rwkv_rk': nrm((DEPTH, B_HEADS, B_HEAD), 0.1),
        'rwkv_lnx_g': 1.0 + nrm((DEPTH, W_B), 0.01),
        'rwkv_lnx_b': nrm((DEPTH, W_B), 0.01),
        'convc_w': nrm((DEPTH, C_CONV, W_C), C_CONV ** -0.5),
        'convd_w': nrm((DEPTH, D_CONV, W_D), D_CONV ** -0.5),
        'convd_b': nrm((DEPTH, W_D), 0.01),
        'lru_gx_w': nrm((DEPTH, D_HEADS, D_BLOCK, D_BLOCK), D_BLOCK ** -0.5),
        'lru_gx_b': nrm((DEPTH, W_D), 0.01),
        'lru_ga_w': nrm((DEPTH, D_HEADS, D_BLOCK, D_BLOCK), D_BLOCK ** -0.5),
        'lru_ga_b': nrm((DEPTH, W_D), 0.01),
        'lru_lam': lru_lam,
        'ln1_g': 1.0 + nrm((DEPTH, D_MODEL), 0.01),
        'ln1_b': nrm((DEPTH, D_MODEL), 0.01),
        'ln2_g': 1.0 + nrm((DEPTH, D_MODEL), 0.01),
        'ln2_b': nrm((DEPTH, D_MODEL), 0.01),
        'w_router': nrm((D_MODEL, N_EXPERTS), D_MODEL ** -0.5),
        'router_bias': nrm((N_EXPERTS,), 0.01),
        'moe_w_gate': nrm((DEPTH, N_EXPERTS, D_MODEL, D_EXPERT), D_MODEL ** -0.5),
        'moe_w_up': nrm((DEPTH, N_EXPERTS, D_MODEL, D_EXPERT), D_MODEL ** -0.5),
        'moe_w_down': nrm((DEPTH, N_EXPERTS, D_EXPERT, D_MODEL), D_EXPERT ** -0.5 * DEEPNORM_BETA),
    }


def reference(x_prompt, x_sample, cache_k, cache_v, cache_ik, page_table, state_rwkv, state_rwkv_shift,
              state_conv_c, state_conv_d, state_lru, w_in, w_out, idx_ln_g, idx_ln_b, rwkv_mu, rwkv_w0,
              rwkv_w2, rwkv_a0, rwkv_a2, rwkv_g2, rwkv_kk, rwkv_ka, rwkv_rk, rwkv_lnx_g, rwkv_lnx_b,
              convc_w, convd_w, convd_b, lru_gx_w, lru_gx_b, lru_ga_w, lru_ga_b, lru_lam,
              ln1_g, ln1_b, ln2_g, ln2_b, w_router, router_bias, moe_w_gate, moe_w_up, moe_w_down):
    bp, tp = x_prompt.shape[0], x_prompt.shape[1]
    ts = x_sample.shape[1]
    dt = x_prompt.dtype
    pos_p = jnp.arange(tp)
    pos_s = PAST_LEN + jnp.arange(ts)
    xp, xs = x_prompt, x_sample
    new_p = [[] for _ in range(8)]
    new_s = [[] for _ in range(8)]
    for l in range(DEPTH):
        lp = {
            'w_in': w_in[l], 'w_out': w_out[l], 'idx_ln_g': idx_ln_g[l], 'idx_ln_b': idx_ln_b[l],
            'rwkv_mu': rwkv_mu[l], 'rwkv_w0': rwkv_w0[l], 'rwkv_w2': rwkv_w2[l], 'rwkv_a0': rwkv_a0[l],
            'rwkv_a2': rwkv_a2[l], 'rwkv_g2': rwkv_g2[l], 'rwkv_kk': rwkv_kk[l], 'rwkv_ka': rwkv_ka[l],
            'rwkv_rk': rwkv_rk[l], 'rwkv_lnx_g': rwkv_lnx_g[l], 'rwkv_lnx_b': rwkv_lnx_b[l],
            'convc_w': convc_w[l], 'convd_w': convd_w[l], 'convd_b': convd_b[l],
            'lru_gx_w': lru_gx_w[l], 'lru_gx_b': lru_gx_b[l], 'lru_ga_w': lru_ga_w[l], 'lru_ga_b': lru_ga_b[l],
            'lru_lam': lru_lam[l], 'ln1_g': ln1_g[l], 'ln1_b': ln1_b[l], 'ln2_g': ln2_g[l], 'ln2_b': ln2_b[l],
            'moe_w_gate': moe_w_gate[l], 'moe_w_up': moe_w_up[l], 'moe_w_down': moe_w_down[l],
        }
        xp, st_p = layer_step(xp, pos_p, lp, dsa_prompt,
                              jnp.zeros((bp, B_HEADS, B_HEAD, B_HEAD), dt), jnp.zeros((bp, 1, B_PROJ), dt),
                              jnp.zeros((bp, C_CONV - 1, W_C), dt), jnp.zeros((bp, D_CONV - 1, W_D), dt),
                              jnp.zeros((bp, W_D), dt), w_router, router_bias)
        past = (gather_pages(cache_k, l, page_table), gather_pages(cache_v, l, page_table),
                gather_pages(cache_ik, l, page_table))
        attn_s = functools.partial(dsa_sample, past=past)
        xs, st_s = layer_step(xs, pos_s, lp, attn_s, state_rwkv[l], state_rwkv_shift[l], state_conv_c[l],
                              state_conv_d[l], state_lru[l], w_router, router_bias)
        for i in range(8):
            new_p[i].append(st_p[i])
            new_s[i].append(st_s[i])
    return (xp, xs,
            jnp.stack(new_p[0]), jnp.stack(new_p[1]), jnp.stack(new_p[2]), jnp.stack(new_p[3]),
            jnp.stack(new_p[4]), jnp.stack(new_p[5]), jnp.stack(new_p[6]), jnp.stack(new_p[7]),
            jnp.stack(new_s[0]), jnp.stack(new_s[1]), jnp.stack(new_s[2]), jnp.stack(new_s[3]),
            jnp.stack(new_s[4]), jnp.stack(new_s[5]), jnp.stack(new_s[6]), jnp.stack(new_s[7]))
```

```python
import functools
import math

import jax
import jax.numpy as jnp
from jax import lax
from jax.experimental import pallas as pl
from jax.experimental.pallas import tpu as pltpu

F32 = jnp.float32
BF16 = jnp.bfloat16
I32 = jnp.int32

A_HEADS = 4
A_HEAD_DIM = 64
ROT_DIM = 16
IDX_HEADS = 4
IDX_DIM = 32
IDX_ROT_DIM = 8
ROPE_THETA = 500000.0
TOPK_MAX = 256
PAGE_SIZE = 128
W_MIX = 256
B_HEAD = 64
B_HEADS = 4
DECAY_LORA = 32
AAA_LORA = 32
GATE_LORA = 64
B_PROJ = 3 * W_MIX + DECAY_LORA + AAA_LORA + GATE_LORA
LNX_EPS = 64e-5
C_CONV = 3
D_CONV = 4
D_HEADS = 4
LRU_C = 8.0
N_EXPERTS = 16
N_GROUPS = 4
EXPERTS_PER_GROUP = 4
D_EXPERT = 256
LN_EPS = 1e-5
DEPTH = 4
DEEPNORM_ALPHA = (2 * DEPTH) ** 0.25

A_COLS = 3 * W_MIX + IDX_HEADS * IDX_DIM + IDX_DIM + IDX_HEADS
A_PAD = 1024

LANES = 128
SUBLANES = 8
VMEM_LIMIT = 56 * 1024 * 1024

INT_MIN = -2147483648
NEG_BIG = -1e30
RWKV_CHUNK = 128


def _cparams(sem):
    return pltpu.CompilerParams(dimension_semantics=sem, vmem_limit_bytes=VMEM_LIMIT)


def _dg(a, b, dims):
    return lax.dot_general(a, b, (dims, ((), ())), preferred_element_type=F32)


NN = ((1,), (0,))
NT = ((1,), (1,))


def _mm(a, b, dims=NN):
    return _dg(a.astype(BF16), b.astype(BF16), dims)


def _split2(x):
    hi = x.astype(BF16)
    lo = (x - hi.astype(F32)).astype(BF16)
    return hi, lo


def _split3(x):
    hi = x.astype(BF16)
    r1 = x - hi.astype(F32)
    mid = r1.astype(BF16)
    lo = (r1 - mid.astype(F32)).astype(BF16)
    return hi, mid, lo


def _mm3(a, b, dims=NN):
    ah, al = _split2(a)
    bh, bl = _split2(b)
    return _dg(ah, bh, dims) + (_dg(ah, bl, dims) + _dg(al, bh, dims))


def _mm_xl(a, b_exact, dims=NN):
    h, m, l = _split3(a)
    bb = b_exact.astype(BF16)
    return _dg(h, bb, dims) + (_dg(m, bb, dims) + _dg(l, bb, dims))


def _mm_xr(a_exact, b, dims=NN):
    h, m, l = _split3(b)
    aa = a_exact.astype(BF16)
    return _dg(aa, h, dims) + (_dg(aa, m, dims) + _dg(aa, l, dims))


def _iota(shape, dim):
    return lax.broadcasted_iota(I32, shape, dim)


def _head_block_ones(n, blk):
    r = _iota((n, n), 0) // blk
    c = _iota((n, n), 1) // blk
    return jnp.where(r == c, 1.0, 0.0).astype(F32)


def _layer_norm_rows(x, g, b):
    mu = jnp.mean(x, axis=-1, keepdims=True)
    d = x - mu
    var = jnp.mean(d * d, axis=-1, keepdims=True)
    return d * lax.rsqrt(var + LN_EPS) * g + b


def _softplus(x):
    return jnp.maximum(x, 0.0) + jnp.log1p(jnp.exp(-jnp.abs(x)))


def _sigmoid(x):
    return 1.0 / (1.0 + jnp.exp(-x))


def _gelu_tanh(x):
    return 0.5 * x * (1.0 + jnp.tanh(math.sqrt(2.0 / math.pi) * (x + 0.044715 * (x * x * x))))


def _rope_block(blk, tab, half):
    c = tab[:, 0:LANES]
    sm = tab[:, LANES:2 * LANES]
    sp = tab[:, 2 * LANES:3 * LANES]
    return blk * c + pltpu.roll(blk, LANES - half, 1) * sm + pltpu.roll(blk, half, 1) * sp


def _inproj_kernel(x_ref, wa_ref, wih_ref, wil_ref, wb_ref, wc_ref, wd_ref, tqk_ref, tqi_ref, tki_ref, ln_ref,
                   q_ref, k_ref, v_ref, qi_ref, kiw_ref, kirep_ref, pb_ref, pc_ref, pd_ref):
    x = x_ref[...]
    xh = x.astype(BF16)
    xl = (x - xh.astype(F32)).astype(BF16)
    acc = _dg(xh, wa_ref[...], NN)
    tqk = tqk_ref[...]
    for j in range(2):
        q_ref[:, j * LANES:(j + 1) * LANES] = _rope_block(acc[:, j * LANES:(j + 1) * LANES], tqk, ROT_DIM // 2)
        k_ref[:, j * LANES:(j + 1) * LANES] = _rope_block(acc[:, (2 + j) * LANES:(3 + j) * LANES], tqk,
                                                          ROT_DIM // 2)
    v_ref[...] = acc[:, 4 * LANES:6 * LANES]
    wih = wih_ref[...]
    acci = _dg(xh, wih, NN) + (_dg(xh, wil_ref[...], NN) + _dg(xl, wih, NN))
    qi_ref[...] = _rope_block(acci[:, 0:LANES], tqi_ref[...], IDX_ROT_DIM // 2)
    kw = acci[:, LANES:2 * LANES]
    lane = _iota(kw.shape, 1)
    in_ki = lane < IDX_DIM
    mu = jnp.sum(jnp.where(in_ki, kw, 0.0), axis=-1, keepdims=True) * (1.0 / IDX_DIM)
    d = jnp.where(in_ki, kw - mu, 0.0)
    var = jnp.sum(d * d, axis=-1, keepdims=True) * (1.0 / IDX_DIM)
    kn = d * lax.rsqrt(var + LN_EPS) * ln_ref[0:1, :] + ln_ref[1:2, :]
    kr = _rope_block(kn, tki_ref[...], IDX_ROT_DIM // 2)
    kr = jnp.where(in_ki, kr, 0.0)
    kiw_ref[...] = jnp.where(in_ki, kr, jnp.where(lane < IDX_DIM + IDX_HEADS, kw * (IDX_HEADS ** -0.5), 0.0))
    kirep_ref[...] = (kr + pltpu.roll(kr, IDX_DIM, 1)) + (pltpu.roll(kr, 2 * IDX_DIM, 1)
                                                           + pltpu.roll(kr, 3 * IDX_DIM, 1))
    pb_ref[...] = _dg(xh, wb_ref[...], NN)
    pc_ref[...] = _dg(xh, wc_ref[...], NN)
    pd_ref[...] = _dg(xh, wd_ref[...], NN)


def _inproj(x2d, lw, tabs, tab_rows, tm):
    n, dm = x2d.shape
    nt = tab_rows // tm
    full = lambda a: pl.BlockSpec(a.shape, lambda i: (0, 0))
    row = lambda w: pl.BlockSpec((tm, w), lambda i: (i, 0))
    tab = lambda: pl.BlockSpec((tm, 3 * LANES), lambda i: (i % nt, 0))
    widths = (W_MIX, W_MIX, W_MIX, LANES, LANES, LANES, B_PROJ, 3 * W_MIX, 2 * W_MIX)
    return pl.pallas_call(
        _inproj_kernel,
        grid=(n // tm,),
        in_specs=[row(dm), full(lw['wa']), full(lw['wih']), full(lw['wil']), full(lw['wb']), full(lw['wc']),
                  full(lw['wd']), tab(), tab(), tab(), full(lw['idx_ln'])],
        out_specs=[row(w) for w in widths],
        out_shape=[jax.ShapeDtypeStruct((n, w), F32) for w in widths],
        compiler_params=_cparams(("parallel",)),
        name="inproj",
    )(x2d, lw['wa'], lw['wih'], lw['wil'], lw['wb'], lw['wc'], lw['wd'], tabs[0], tabs[1], tabs[2], lw['idx_ln'])


def _float_order_key(score):
    score = jnp.where(score == 0.0, 0.0, score)
    bits = pltpu.bitcast(score, I32)
    return jnp.where(bits < 0, bits ^ jnp.int32(0x7FFFFFFF), bits)


def _kth_largest_key(count_ge, topk, shape):
    kf = jnp.float32(topk)
    thr0 = jnp.where(count_ge(jnp.zeros(shape, I32)) >= kf, jnp.int32(0), jnp.int32(INT_MIN))

    def body(i, thr):
        cand = thr + jnp.left_shift(jnp.int32(1), jnp.int32(30) - i)
        return jnp.where(count_ge(cand) >= kf, cand, thr)

    return lax.fori_loop(0, 31, body, thr0)


def _upper_tri_ones(n):
    return jnp.where(_iota((n, n), 0) <= _iota((n, n), 1), 1.0, 0.0).astype(BF16)


def _select_bias(key_ref, adm_of_chunk, bias_ref, topk, n_chunks, rows):
    def count_ge(c):
        return jnp.sum(jnp.where(key_ref[...] >= c, 1.0, 0.0), axis=1, keepdims=True)

    thr = _kth_largest_key(count_ge, topk, (rows, 1))
    n_gt = jnp.sum(jnp.where(key_ref[...] > thr, 1.0, 0.0), axis=1, keepdims=True)
    need = jnp.float32(topk) - n_gt
    tri = _upper_tri_ones(LANES)
    off = jnp.zeros((rows, 1), F32)
    for c in range(n_chunks):
        kc = key_ref[:, c * LANES:(c + 1) * LANES]
        eq = jnp.where(kc == thr, jnp.where(adm_of_chunk(c), 1.0, 0.0), 0.0)
        pre = _dg(eq.astype(BF16), tri, NN)
        take_tie = jnp.where((pre + off) <= need, eq, 0.0)
        sel = jnp.where(kc > thr, 1.0, take_tie)
        bias_ref[:, c * LANES:(c + 1) * LANES] = jnp.where(sel > 0.5, 0.0, NEG_BIG)
        off = off + pre[:, LANES - 1:LANES]


def _attn_prompt_kernel(topk, q_ref, qi_ref, kiwq_ref, k_ref, v_ref, kirep_ref, o_ref, key_scr, bias_scr):
    qb = q_ref.shape[0]
    t_keys = k_ref.shape[0]
    j = pl.program_id(1)
    qi = qi_ref[...]
    kh, kl = _split2(kirep_ref[...])
    lane = _iota((1, LANES), 1)
    score = jnp.zeros((qb, t_keys), F32)
    for h in range(IDX_HEADS):
        qh = jnp.where(lane // IDX_DIM == h, qi, 0.0)
        qhh, qhl = _split2(qh)
        dots = _dg(qhh, kh, NT) + (_dg(qhh, kl, NT) + _dg(qhl, kh, NT))
        wih = kiwq_ref[:, IDX_DIM + h:IDX_DIM + h + 1]
        score = score + wih * jnp.maximum(dots * (IDX_DIM ** -0.5), 0.0)
    kidx = _iota((qb, t_keys), 1)
    qpos = j * qb + _iota((qb, t_keys), 0)
    adm = kidx <= qpos
    key_scr[...] = jnp.where(adm, _float_order_key(score), jnp.int32(INT_MIN))

    def adm_of_chunk(c):
        return (c * LANES + _iota((qb, LANES), 1)) <= (j * qb + _iota((qb, LANES), 0))

    _select_bias(key_scr, adm_of_chunk, bias_scr, topk, t_keys // LANES, qb)

    q = q_ref[...]
    kb = k_ref[...].astype(BF16)
    vb = v_ref[...].astype(BF16)
    lane_q = _iota((1, A_HEADS * A_HEAD_DIM), 1)
    out = jnp.zeros(q.shape, F32)
    for h in range(A_HEADS):
        mh = lane_q // A_HEAD_DIM == h
        qh = jnp.where(mh, q, 0.0).astype(BF16)
        lg = _dg(qh, kb, NT) * (A_HEAD_DIM ** -0.5) + bias_scr[...]
        m = jnp.max(lg, axis=1, keepdims=True)
        p = jnp.exp(lg - m)
        den = jnp.sum(p, axis=1, keepdims=True)
        oh = _dg(p.astype(BF16), vb, NN)
        out = out + jnp.where(mh, oh / den, 0.0)
    o_ref[...] = out


def _attn_prompt(q, qi, kiw, k, v, kirep, b, t):
    qb = LANES
    nq = t // qb
    topk = min(TOPK_MAX, t // 4)
    qspec = lambda w: pl.BlockSpec((qb, w), lambda bi, j: (bi * nq + j, 0))
    kspec = lambda w: pl.BlockSpec((t, w), lambda bi, j: (bi, 0))
    return pl.pallas_call(
        functools.partial(_attn_prompt_kernel, topk),
        grid=(b, nq),
        in_specs=[qspec(W_MIX), qspec(LANES), qspec(LANES), kspec(W_MIX), kspec(W_MIX), kspec(LANES)],
        out_specs=qspec(W_MIX),
        out_shape=jax.ShapeDtypeStruct((b * t, W_MIX), F32),
        scratch_shapes=[pltpu.VMEM((qb, t), I32), pltpu.VMEM((qb, t), F32)],
        compiler_params=_cparams(("parallel", "arbitrary")),
        name="attn_prompt",
    )(q, qi, kiw, k, v, kirep)


def _attn_sample_kernel(topk, n_pages, t_new, pt_ref, qm_ref, qst_ref, wst_ref, knew_ref, vnew_ref, kinew_ref,
                        *rest):
    k_pages = rest[0:n_pages]
    v_pages = rest[n_pages:2 * n_pages]
    ik_pages = rest[2 * n_pages:3 * n_pages]
    o_ref, key_scr, bias_scr, lg_scr = rest[3 * n_pages:]
    del pt_ref
    rows = SUBLANES
    n_chunks = n_pages + 1
    pad_rows = LANES - SUBLANES

    def chunk_keys(ref_list, new_ref, c):
        if c < n_pages:
            return ref_list[c][...]
        new = new_ref[...]
        return jnp.concatenate([new, jnp.zeros((pad_rows, new.shape[1]), F32)], axis=0)

    def adm_of_chunk(c):
        if c < n_pages:
            return jnp.full((rows, LANES), True)
        r = _iota((rows, LANES), 0)
        l = _iota((rows, LANES), 1)
        return l <= jnp.minimum(r, t_new - 1)

    qst = qst_ref[...]
    qsh, qsl = _split2(qst)
    wst = wst_ref[...]
    for c in range(n_chunks):
        ki = chunk_keys(ik_pages, kinew_ref, c)
        kih, kil = _split2(ki)
        dots = _dg(qsh, kih, NT) + (_dg(qsh, kil, NT) + _dg(qsl, kih, NT))
        sc = wst * jnp.maximum(dots * (IDX_DIM ** -0.5), 0.0)
        score = (sc[0:rows] + sc[rows:2 * rows]) + (sc[2 * rows:3 * rows] + sc[3 * rows:4 * rows])
        key_scr[:, c * LANES:(c + 1) * LANES] = jnp.where(adm_of_chunk(c), _float_order_key(score),
                                                          jnp.int32(INT_MIN))
    _select_bias(key_scr, adm_of_chunk, bias_scr, topk, n_chunks, rows)

    qm = qm_ref[...].astype(BF16)
    for c in range(n_chunks):
        kc = chunk_keys(k_pages, knew_ref, c).astype(BF16)
        bias = bias_scr[:, c * LANES:(c + 1) * LANES]
        bias4 = jnp.concatenate([bias] * A_HEADS, axis=0)
        lg_scr[:, c * LANES:(c + 1) * LANES] = _dg(qm, kc, NT) * (A_HEAD_DIM ** -0.5) + bias4
    lg = lg_scr[...]
    m = jnp.max(lg, axis=1, keepdims=True)
    lg_scr[...] = jnp.exp(lg - m)
    den = jnp.sum(lg_scr[...], axis=1, keepdims=True)
    acc = jnp.zeros((A_HEADS * rows, A_HEADS * A_HEAD_DIM), F32)
    for c in range(n_chunks):
        vc = chunk_keys(v_pages, vnew_ref, c).astype(BF16)
        acc = acc + _dg(lg_scr[:, c * LANES:(c + 1) * LANES].astype(BF16), vc, NN)
    acc = acc / den
    lane_q = _iota((1, A_HEADS * A_HEAD_DIM), 1)
    out = jnp.zeros((rows, A_HEADS * A_HEAD_DIM), F32)
    for h in range(A_HEADS):
        out = out + jnp.where(lane_q // A_HEAD_DIM == h, acc[h * rows:(h + 1) * rows], 0.0)
    o_ref[...] = out


def _attn_sample(layer, q, qi, kiw, k_new, v_new, page_table, cache_k, cache_v, cache_ik):
    bd, ts, _ = q.shape
    n_pages = page_table.shape[1]
    n_phys = cache_k.shape[1]
    rows = SUBLANES
    topk = min(TOPK_MAX, (n_pages * PAGE_SIZE + ts) // 4)
    padt = lambda a: jnp.pad(a, ((0, 0), (0, rows - ts), (0, 0)))
    qp = padt(q)
    head_of_lane = jnp.arange(A_HEADS * A_HEAD_DIM) // A_HEAD_DIM
    qm = jnp.where(head_of_lane[None, None, None, :] == jnp.arange(A_HEADS)[None, :, None, None],
                   qp[:, None, :, :], 0.0).reshape(bd, A_HEADS * rows, A_HEADS * A_HEAD_DIM)
    qst = padt(qi).reshape(bd, rows, IDX_HEADS, IDX_DIM).transpose(0, 2, 1, 3).reshape(bd, IDX_HEADS * rows, IDX_DIM)
    wi = padt(kiw[:, :, IDX_DIM:IDX_DIM + IDX_HEADS])
    wst = jnp.broadcast_to(wi.transpose(0, 2, 1).reshape(bd, IDX_HEADS * rows, 1), (bd, IDX_HEADS * rows, LANES))
    kp, vp = padt(k_new), padt(v_new)
    kinew = padt(kiw[:, :, 0:IDX_DIM])
    ck = cache_k.reshape(cache_k.shape[0], n_phys, PAGE_SIZE, A_HEADS * A_HEAD_DIM)
    cv = cache_v.reshape(cache_v.shape[0], n_phys, PAGE_SIZE, A_HEADS * A_HEAD_DIM)
    per_b = lambda r, w: pl.BlockSpec((None, r, w), lambda bi, pt: (bi, 0, 0))

    def page_spec(p, w):
        return pl.BlockSpec((None, None, PAGE_SIZE, w), lambda bi, pt: (layer, pt[bi * n_pages + p], 0, 0))

    in_specs = [per_b(A_HEADS * rows, W_MIX), per_b(IDX_HEADS * rows, IDX_DIM), per_b(IDX_HEADS * rows, LANES),
                per_b(rows, W_MIX), per_b(rows, W_MIX), per_b(rows, IDX_DIM)]
    in_specs += [page_spec(p, W_MIX) for p in range(n_pages)]
    in_specs += [page_spec(p, W_MIX) for p in range(n_pages)]
    in_specs += [page_spec(p, IDX_DIM) for p in range(n_pages)]
    width = (n_pages + 1) * LANES
    out = pl.pallas_call(
        functools.partial(_attn_sample_kernel, topk, n_pages, ts),
        grid_spec=pltpu.PrefetchScalarGridSpec(
            num_scalar_prefetch=1,
            grid=(bd,),
            in_specs=in_specs,
            out_specs=pl.BlockSpec((None, rows, W_MIX), lambda bi, pt: (bi, 0, 0)),
            scratch_shapes=[pltpu.VMEM((rows, width), I32), pltpu.VMEM((rows, width), F32),
                            pltpu.VMEM((A_HEADS * rows, width), F32)]),
        out_shape=jax.ShapeDtypeStruct((bd, rows, W_MIX), F32),
        compiler_params=_cparams(("arbitrary",)),
        name="attn_sample",
    )(page_table.reshape(-1), qm, qst, wst, kp, vp, kinew,
      *([ck] * n_pages), *([cv] * n_pages), *([cache_ik] * n_pages))
    return out[:, :ts]


def _rwkv_prep_math(p, prev, mu, w6, vecs, r_ref, lw_ref, k_ref, v_ref, a_ref, b_ref, g_ref):
    z = p + (prev - p) * mu
    r = z[:, 0:W_MIX]
    k = z[:, W_MIX:2 * W_MIX]
    v = z[:, 2 * W_MIX:3 * W_MIX]
    x6 = z[:, 3 * W_MIX:B_PROJ]
    lane = _iota(x6.shape, 1)
    act = jnp.where(lane < DECAY_LORA, jnp.tanh(x6),
                    jnp.where(lane < DECAY_LORA + AAA_LORA, x6, _sigmoid(x6)))
    lo = _mm(act, w6)
    w0, a0, kkw, ka = vecs[0:1], vecs[1:2], vecs[2:3], vecs[3:4]
    w_log = -_softplus(-(w0 + lo[:, 0:W_MIX])) - 0.5
    asig = _sigmoid(a0 + lo[:, W_MIX:2 * W_MIX])
    kk = k * kkw
    ss = _mm_xl(kk * kk, _head_block_ones(W_MIX, B_HEAD))
    kk = kk / jnp.maximum(jnp.sqrt(ss), 1e-12)
    r_ref[...] = r
    lw_ref[...] = -jnp.exp(w_log)
    k_ref[...] = k * (1.0 + (asig - 1.0) * ka)
    v_ref[...] = v
    a_ref[...] = -kk
    b_ref[...] = kk * asig
    g_ref[...] = lo[:, 2 * W_MIX:3 * W_MIX]


def _rwkv_prep_prompt_kernel(tiles_per_seq, p_ref, pprev_ref, mu_ref, w6_ref, vec_ref, *rest):
    outs, scr = rest[:7], rest[7]
    tm = p_ref.shape[0]
    p = p_ref[...]
    first = jnp.where(pl.program_id(0) % tiles_per_seq == 0, 0.0, pprev_ref[SUBLANES - 1:SUBLANES, :])
    scr[SUBLANES - 1:SUBLANES, :] = first
    scr[SUBLANES:SUBLANES + tm, :] = p
    prev = scr[SUBLANES - 1:SUBLANES - 1 + tm, :]
    _rwkv_prep_math(p, prev, mu_ref[...], w6_ref[...], vec_ref[...], *outs)


def _rwkv_prep_sample_kernel(bd, p_ref, shift_ref, mu_ref, w6_ref, vec_ref, *outs):
    p = p_ref[...]
    prev = jnp.concatenate([shift_ref[...], p[0:p.shape[0] - bd]], axis=0)
    _rwkv_prep_math(p, prev, mu_ref[...], w6_ref[...], vec_ref[...], *outs)


def _rwkv_prep_prompt(pb, lw, t, tm):
    n = pb.shape[0]
    tps = t // tm
    full = lambda a: pl.BlockSpec(a.shape, lambda i: (0, 0))
    row = lambda w: pl.BlockSpec((tm, w), lambda i: (i, 0))
    prev_spec = pl.BlockSpec((SUBLANES, B_PROJ), lambda i: (jnp.maximum(i * (tm // SUBLANES) - 1, 0), 0))
    return pl.pallas_call(
        functools.partial(_rwkv_prep_prompt_kernel, tps),
        grid=(n // tm,),
        in_specs=[row(B_PROJ), prev_spec, full(lw['rwkv_mu']), full(lw['w6']), full(lw['rwkv_vec'])],
        out_specs=[row(W_MIX)] * 7,
        out_shape=[jax.ShapeDtypeStruct((n, W_MIX), F32)] * 7,
        scratch_shapes=[pltpu.VMEM((tm + SUBLANES, B_PROJ), F32)],
        compiler_params=_cparams(("parallel",)),
        name="rwkv_prep_prompt",
    )(pb, pb, lw['rwkv_mu'], lw['w6'], lw['rwkv_vec'])


def _rwkv_prep_sample(pb, shift, lw, bd):
    n = pb.shape[0]
    full = lambda a: pl.BlockSpec(a.shape, lambda i: (0, 0))
    return pl.pallas_call(
        functools.partial(_rwkv_prep_sample_kernel, bd),
        grid=(1,),
        in_specs=[full(pb), full(shift), full(lw['rwkv_mu']), full(lw['w6']), full(lw['rwkv_vec'])],
        out_specs=[pl.BlockSpec((n, W_MIX), lambda i: (0, 0))] * 7,
        out_shape=[jax.ShapeDtypeStruct((n, W_MIX), F32)] * 7,
        compiler_params=_cparams(("arbitrary",)),
        name="rwkv_prep_sample",
    )(pb, shift, lw['rwkv_mu'], lw['w6'], lw['rwkv_vec'])


def _rwkv_post(y, r, k, v, g, vecs, jhead):
    lnx_g, lnx_b, rk = vecs[4:5], vecs[5:6], vecs[6:7]
    mu = _mm_xl(y, jhead) * (1.0 / B_HEAD)
    d = y - mu
    var = _mm_xl(d * d, jhead) * (1.0 / B_HEAD)
    yn = d * lax.rsqrt(var + LNX_EPS) * lnx_g + lnx_b
    bonus = _mm_xl(r * k * rk, jhead) * v
    return (yn + bonus) * g


def _rwkv_chunk_kernel(gb, r_ref, lw_ref, k_ref, v_ref, a_ref, b_ref, g_ref, vec_ref, y_ref, sfin_ref, s_scr):
    c = r_ref.shape[1]
    w = W_MIX
    ci = pl.program_id(1)

    @pl.when(ci == 0)
    def _():
        s_scr[...] = jnp.zeros(s_scr.shape, F32)

    row_c = _iota((c, c), 0)
    col_c = _iota((c, c), 1)
    incl = row_c >= col_c
    strict = row_c > col_c
    ltri = jnp.where(incl, 1.0, 0.0).astype(BF16)
    eye_c = jnp.where(row_c == col_c, 1.0, 0.0).astype(F32)
    jhead = _head_block_ones(w, B_HEAD)
    bm = jhead > 0.5
    eye_w = _iota((w, w), 0) == _iota((w, w), 1)
    lane = _iota((1, w), 1)
    vecs = vec_ref[...]
    levels = int(math.log2(c))

    for bi in range(gb):
        r, lw, k, v, a, b = r_ref[bi], lw_ref[bi], k_ref[bi], v_ref[bi], a_ref[bi], b_ref[bi]
        cum = _mm_xr(ltri, lw)
        last = cum[c - 1:c, :]
        p_in = jnp.exp(cum)
        p_inv = jnp.exp(-cum)
        p_end = jnp.exp(last - cum)
        at = a * jnp.exp(cum - lw)
        rt = r * p_in
        bt = (b * p_inv).astype(BF16)
        kt = (k * p_inv).astype(BF16)
        bp = b * p_end
        kp = k * p_end
        vb = v.astype(BF16)
        wt = jnp.zeros((c, w), F32)
        u0 = jnp.zeros((c, w), F32)
        brb_h, brk_h = [], []
        for h in range(B_HEADS):
            mh = lane // B_HEAD == h
            ar = jnp.concatenate([jnp.where(mh, at, 0.0), jnp.where(mh, rt, 0.0)], axis=0).astype(BF16)
            mb = _dg(ar, bt, NT)
            mk = _dg(ar, kt, NT)
            n_mat = jnp.where(strict, mb[0:c], 0.0)
            aak = jnp.where(strict, mk[0:c], 0.0)
            brb_h.append(jnp.where(incl, mb[c:2 * c], 0.0).astype(BF16))
            brk_h.append(jnp.where(incl, mk[c:2 * c], 0.0).astype(BF16))
            tm_ = eye_c + n_mat
            npow = n_mat
            for _ in range(levels - 1):
                npow = _mm(npow, npow)
                tm_ = tm_ + _mm(tm_, npow)
            tmb = tm_.astype(BF16)
            wt = wt + jnp.where(mh, _dg(tmb, at.astype(BF16), NN), 0.0)
            x = _dg(aak.astype(BF16), vb, NN)
            u0 = u0 + jnp.where(mh, _dg(tmb, x.astype(BF16), NN), 0.0)
        wtb = wt.astype(BF16)
        u0b = u0.astype(BF16)
        ry = rt
        y0 = jnp.zeros((c, w), F32)
        for h in range(B_HEADS):
            mh = lane // B_HEAD == h
            ry = ry + jnp.where(mh, _dg(brb_h[h], wtb, NN), 0.0)
            y0 = y0 + jnp.where(mh, _dg(brb_h[h], u0b, NN) + _dg(brk_h[h], vb, NN), 0.0)
        bpt = bp.T.astype(BF16)
        kpt = kp.T.astype(BF16)
        gt = jnp.where(eye_w, jnp.exp(last), 0.0) + jnp.where(bm, _dg(bpt, wtb, NN), 0.0)
        ht = jnp.where(bm, _dg(bpt, u0b, NN) + _dg(kpt, vb, NN), 0.0)
        s0 = s_scr[bi]
        y = _mm3(ry, s0) + y0
        s_scr[bi] = _mm3(gt, s0) + ht
        y_ref[bi] = _rwkv_post(y, r, k, v, g_ref[bi], vecs, jhead)

    @pl.when(ci == pl.num_programs(1) - 1)
    def _():
        fold = jnp.where(_iota((w, B_HEAD), 0) % B_HEAD == _iota((w, B_HEAD), 1), 1.0, 0.0)
        for bi in range(gb):
            sfin_ref[bi] = _mm_xl(s_scr[bi], fold)


def _rwkv_scan_prompt(arrs, lw, b, t, gb):
    c = RWKV_CHUNK
    a3 = [x.reshape(b, t, W_MIX) for x in arrs]
    blk = pl.BlockSpec((gb, c, W_MIX), lambda i, j: (i, j, 0))
    y, sfin = pl.pallas_call(
        functools.partial(_rwkv_chunk_kernel, gb),
        grid=(b // gb, t // c),
        in_specs=[blk] * 7 + [pl.BlockSpec(lw['rwkv_vec'].shape, lambda i, j: (0, 0))],
        out_specs=[blk, pl.BlockSpec((gb, W_MIX, B_HEAD), lambda i, j: (i, 0, 0))],
        out_shape=[jax.ShapeDtypeStruct((b, t, W_MIX), F32), jax.ShapeDtypeStruct((b, W_MIX, B_HEAD), F32)],
        scratch_shapes=[pltpu.VMEM((gb, W_MIX, W_MIX), F32)],
        compiler_params=_cparams(("parallel", "arbitrary")),
        name="rwkv_chunk",
    )(*a3, lw['rwkv_vec'])
    state = sfin.reshape(b, B_HEADS, B_HEAD, B_HEAD).transpose(0, 1, 3, 2)
    return y.reshape(b * t, W_MIX), state


def _rwkv_step_kernel(gb, r_ref, lw_ref, k_ref, v_ref, a_ref, b_ref, g_ref, s_ref, vec_ref, y_ref, so_ref, y_scr):
    ts = r_ref.shape[0]
    w = W_MIX
    jhead = _head_block_ones(w, B_HEAD)
    q1 = jnp.where(_iota((B_HEAD, w), 0) == _iota((B_HEAD, w), 1) % B_HEAD, 1.0, 0.0)
    for bi in range(gb):
        s = s_ref[bi]
        for t in range(ts):
            row = lambda ref: ref[t, bi:bi + 1, :]
            sa = _mm_xl(s * row(a_ref), jhead)
            vcol = _mm_xl(q1 * row(v_ref), jhead)
            s = s * jnp.exp(row(lw_ref)) + sa * row(b_ref) + vcol * row(k_ref)
            yb = _mm_xl(s * row(r_ref), jhead)
            y_scr[t, bi:bi + 1, :] = jnp.sum(yb * q1, axis=0, keepdims=True)
        so_ref[bi] = s
    vecs = vec_ref[...]
    for t in range(ts):
        y_ref[t] = _rwkv_post(y_scr[t], r_ref[t], k_ref[t], v_ref[t], g_ref[t], vecs, jhead)


def _rwkv_scan_sample(arrs, state, lw, bd, ts, gb):
    a3 = [x.reshape(ts, bd, W_MIX) for x in arrs]
    s_in = state.transpose(0, 2, 1, 3).reshape(bd, B_HEAD, W_MIX)
    blk = pl.BlockSpec((ts, gb, W_MIX), lambda i: (0, i, 0))
    sblk = pl.BlockSpec((gb, B_HEAD, W_MIX), lambda i: (i, 0, 0))
    y, s_out = pl.pallas_call(
        functools.partial(_rwkv_step_kernel, gb),
        grid=(bd // gb,),
        in_specs=[blk] * 7 + [sblk, pl.BlockSpec(lw['rwkv_vec'].shape, lambda i: (0, 0))],
        out_specs=[blk, sblk],
        out_shape=[jax.ShapeDtypeStruct((ts, bd, W_MIX), F32), jax.ShapeDtypeStruct((bd, B_HEAD, W_MIX), F32)],
        scratch_shapes=[pltpu.VMEM((ts, gb, W_MIX), F32)],
        compiler_params=_cparams(("parallel",)),
        name="rwkv_step",
    )(*a3, s_in, lw['rwkv_vec'])
    new_state = s_out.reshape(bd, B_HEAD, B_HEADS, B_HEAD).transpose(0, 2, 1, 3)
    return y.reshape(ts * bd, W_MIX), new_state


def _lru_inputs(xc, gate_w_ref, vec):
    gates = _mm(xc, gate_w_ref[...])
    gate_x = _sigmoid(gates[:, 0:W_MIX] + vec[0:1])
    gate_a = _sigmoid(gates[:, W_MIX:2 * W_MIX] + vec[1:2])
    log_a = -LRU_C * gate_a * _softplus(-vec[2:3])
    a2 = jnp.exp(2.0 * log_a)
    neg_expm1 = -jnp.tanh(log_a) * (a2 + 1.0)
    u = jnp.sqrt(neg_expm1) * gate_x * xc
    return jnp.exp(log_a), u


def _cd_prompt_kernel(pc_ref, pd_ref, cw_ref, dw_ref, gw_ref, vec_ref,
                      yc_ref, yd_ref, bufc_ref, bufd_ref, h_ref, gx_scr, xb_scr, a_scr, u_scr, hs_scr, h_scr):
    tt = pc_ref.shape[0]
    ti = pl.program_id(1)
    pad = SUBLANES

    @pl.when(ti == 0)
    def _():
        gx_scr[0:pad, :] = jnp.zeros((pad, W_MIX), F32)
        xb_scr[0:pad, :] = jnp.zeros((pad, W_MIX), F32)
        h_scr[...] = jnp.zeros(h_scr.shape, F32)

    @pl.when(ti > 0)
    def _():
        gx_scr[0:pad, :] = gx_scr[tt:tt + pad, :]
        xb_scr[0:pad, :] = xb_scr[tt:tt + pad, :]

    vec = vec_ref[...]
    gx_scr[pad:pad + tt, :] = pc_ref[:, W_MIX:2 * W_MIX] * pc_ref[:, 2 * W_MIX:3 * W_MIX]
    conv = cw_ref[0:1, :] * gx_scr[pad - 2:pad - 2 + tt, :]
    for j in range(1, C_CONV):
        conv = conv + cw_ref[j:j + 1, :] * gx_scr[pad - 2 + j:pad - 2 + j + tt, :]
    yc_ref[...] = pc_ref[:, 0:W_MIX] * conv
    bufc_ref[...] = gx_scr[pad + tt - (C_CONV - 1):pad + tt, :]
    xb_scr[pad:pad + tt, :] = pd_ref[:, 0:W_MIX]
    xc = dw_ref[0:1, :] * xb_scr[pad - 3:pad - 3 + tt, :]
    for j in range(1, D_CONV):
        xc = xc + dw_ref[j:j + 1, :] * xb_scr[pad - 3 + j:pad - 3 + j + tt, :]
    xc = xc + vec[3:4]
    bufd_ref[...] = xb_scr[pad + tt - (D_CONV - 1):pad + tt, :]
    a, u = _lru_inputs(xc, gw_ref, vec)
    a_scr[...] = a
    u_scr[...] = u
    row8 = _iota((SUBLANES, W_MIX), 0)

    def group(gi, h):
        base = pl.multiple_of(gi * SUBLANES, SUBLANES)
        a8 = a_scr[pl.ds(base, SUBLANES), :]
        u8 = u_scr[pl.ds(base, SUBLANES), :]
        hs8 = jnp.zeros((SUBLANES, W_MIX), F32)
        for j in range(SUBLANES):
            h = a8[j:j + 1, :] * h + u8[j:j + 1, :]
            hs8 = jnp.where(row8 == j, h, hs8)
        hs_scr[pl.ds(base, SUBLANES), :] = hs8
        return h

    h = lax.fori_loop(0, tt // SUBLANES, group, h_scr[...])
    h_scr[...] = h
    h_ref[...] = h
    yd_ref[...] = hs_scr[...] * _gelu_tanh(pd_ref[:, W_MIX:2 * W_MIX])


def _cd_prompt(pc, pd, lw, b, t, tt):
    nt = t // tt
    row = lambda w: pl.BlockSpec((tt, w), lambda bi, ti: (bi * nt + ti, 0))
    full = lambda a: pl.BlockSpec(a.shape, lambda bi, ti: (0, 0))
    per_b = lambda r: pl.BlockSpec((None, r, W_MIX), lambda bi, ti: (bi, 0, 0))
    big = lambda: pltpu.VMEM((tt + 2 * SUBLANES, W_MIX), F32)
    tile = lambda: pltpu.VMEM((tt, W_MIX), F32)
    return pl.pallas_call(
        _cd_prompt_kernel,
        grid=(b, nt),
        in_specs=[row(3 * W_MIX), row(2 * W_MIX), full(lw['convc_w']), full(lw['convd_w']), full(lw['lru_gw']),
                  full(lw['lru_vec'])],
        out_specs=[row(W_MIX), row(W_MIX), per_b(C_CONV - 1), per_b(D_CONV - 1), per_b(1)],
        out_shape=[jax.ShapeDtypeStruct((b * t, W_MIX), F32), jax.ShapeDtypeStruct((b * t, W_MIX), F32),
                   jax.ShapeDtypeStruct((b, C_CONV - 1, W_MIX), F32),
                   jax.ShapeDtypeStruct((b, D_CONV - 1, W_MIX), F32), jax.ShapeDtypeStruct((b, 1, W_MIX), F32)],
        scratch_shapes=[big(), big(), tile(), tile(), tile(), pltpu.VMEM((1, W_MIX), F32)],
        compiler_params=_cparams(("parallel", "arbitrary")),
        name="cd_prompt",
    )(pc, pd, lw['convc_w'], lw['convd_w'], lw['lru_gw'], lw['lru_vec'])


def _cd_sample_kernel(pc_ref, pd_ref, bc_ref, bd_ref, h0_ref, cw_ref, dw_ref, gw_ref, vec_ref,
                      yc_ref, yd_ref, bufc_ref, bufd_ref, h_ref):
    ts = pc_ref.shape[0]
    vec = vec_ref[...]
    gx = [bc_ref[j] for j in range(C_CONV - 1)]
    gx += [pc_ref[t][:, W_MIX:2 * W_MIX] * pc_ref[t][:, 2 * W_MIX:3 * W_MIX] for t in range(ts)]
    for t in range(ts):
        conv = cw_ref[0:1, :] * gx[t]
        for j in range(1, C_CONV):
            conv = conv + cw_ref[j:j + 1, :] * gx[t + j]
        yc_ref[t] = pc_ref[t][:, 0:W_MIX] * conv
    for j in range(C_CONV - 1):
        bufc_ref[j] = gx[ts + j]
    xb = [bd_ref[j] for j in range(D_CONV - 1)] + [pd_ref[t][:, 0:W_MIX] for t in range(ts)]
    h = h0_ref[...]
    for t in range(ts):
        xc = dw_ref[0:1, :] * xb[t]
        for j in range(1, D_CONV):
            xc = xc + dw_ref[j:j + 1, :] * xb[t + j]
        xc = xc + vec[3:4]
        a, u = _lru_inputs(xc, gw_ref, vec)
        h = a * h + u
        yd_ref[t] = h * _gelu_tanh(pd_ref[t][:, W_MIX:2 * W_MIX])
    for j in range(D_CONV - 1):
        bufd_ref[j] = xb[ts + j]
    h_ref[...] = h


def _cd_sample(pc, pd, buf_c, buf_d, h0, lw, bd, ts):
    args = (pc, pd, buf_c, buf_d, h0, lw['convc_w'], lw['convd_w'], lw['lru_gw'], lw['lru_vec'])
    full = lambda a: pl.BlockSpec(a.shape, lambda i: (0,) * a.ndim)
    shapes = [(ts, bd, W_MIX), (ts, bd, W_MIX), (C_CONV - 1, bd, W_MIX), (D_CONV - 1, bd, W_MIX), (bd, W_MIX)]
    return pl.pallas_call(
        _cd_sample_kernel,
        grid=(1,),
        in_specs=[full(a) for a in args],
        out_specs=[pl.BlockSpec(s, lambda i, n=len(s): (0,) * n) for s in shapes],
        out_shape=[jax.ShapeDtypeStruct(s, F32) for s in shapes],
        compiler_params=_cparams(("arbitrary",)),
        name="cd_sample",
    )(*args)


def _outproj_kernel(x_ref, ya_ref, yb_ref, yc_ref, yd_ref, w_ref, ln_ref, o_ref):
    mix = _dg(ya_ref[...].astype(BF16), w_ref[0:W_MIX, :], NN)
    mix = mix + _dg(yb_ref[...].astype(BF16), w_ref[W_MIX:2 * W_MIX, :], NN)
    mix = mix + _dg(yc_ref[...].astype(BF16), w_ref[2 * W_MIX:3 * W_MIX, :], NN)
    mix = mix + _dg(yd_ref[...].astype(BF16), w_ref[3 * W_MIX:4 * W_MIX, :], NN)
    o_ref[...] = _layer_norm_rows(DEEPNORM_ALPHA * x_ref[...] + mix, ln_ref[0:1, :], ln_ref[1:2, :])


def _outproj(x2d, ys, lw, tm):
    n, dm = x2d.shape
    row = lambda w: pl.BlockSpec((tm, w), lambda i: (i, 0))
    full = lambda a: pl.BlockSpec(a.shape, lambda i: (0, 0))
    return pl.pallas_call(
        _outproj_kernel,
        grid=(n // tm,),
        in_specs=[row(dm)] + [row(W_MIX)] * 4 + [full(lw['w_out']), full(lw['ln1'])],
        out_specs=row(dm),
        out_shape=jax.ShapeDtypeStruct((n, dm), F32),
        compiler_params=_cparams(("parallel",)),
        name="outproj",
    )(x2d, *ys, lw['w_out'], lw['ln1'])


def _route(x, wr_h, wr_l, bias_col):
    xh, xl = _split2(x)
    logits = _dg(xh, wr_h, NN) + (_dg(xh, wr_l, NN) + _dg(xl, wr_h, NN))
    st = _sigmoid(logits.T[0:N_EXPERTS, :])
    sel = st + bias_col
    rows = [sel[e:e + 1, :] for e in range(N_EXPERTS)]
    in_top2 = []
    for e in range(N_EXPERTS):
        g0 = (e // EXPERTS_PER_GROUP) * EXPERTS_PER_GROUP
        rank = jnp.zeros(rows[e].shape, F32)
        for o in range(g0, g0 + EXPERTS_PER_GROUP):
            if o == e:
                continue
            ahead = (rows[o] >= rows[e]) if o < e else (rows[o] > rows[e])
            rank = rank + jnp.where(ahead, 1.0, 0.0)
        in_top2.append(rank < float(2))
    gscore = []
    for g in range(N_GROUPS):
        acc = jnp.zeros(rows[0].shape, F32)
        for e in range(g * EXPERTS_PER_GROUP, (g + 1) * EXPERTS_PER_GROUP):
            acc = acc + jnp.where(in_top2[e], rows[e], 0.0)
        gscore.append(acc)
    best = gscore[0]
    gidx = jnp.zeros(best.shape, I32)
    for g in range(1, N_GROUPS):
        better = gscore[g] > best
        best = jnp.where(better, gscore[g], best)
        gidx = jnp.where(better, g, gidx)
    picked = [jnp.where(in_top2[e] & (gidx == e // EXPERTS_PER_GROUP), st[e:e + 1, :], 0.0)
              for e in range(N_EXPERTS)]
    total = picked[0]
    for e in range(1, N_EXPERTS):
        total = total + picked[e]
    return jnp.concatenate(picked, axis=0) / total


def _moe_kernel(x_ref, wrh_ref, wrl_ref, rb_ref, wgu_ref, wd_ref, ln_ref, o_ref, xb_scr, gate_scr, acc_scr):
    e = pl.program_id(1)

    @pl.when(e == 0)
    def _():
        x = x_ref[...]
        xb_scr[...] = x.astype(BF16)
        gates = _route(x, wrh_ref[...], wrl_ref[...], rb_ref[...])
        pad = jnp.zeros((LANES - N_EXPERTS, gates.shape[1]), F32)
        gate_scr[...] = jnp.concatenate([gates, pad], axis=0).T
        acc_scr[...] = jnp.zeros(acc_scr.shape, F32)

    gu = _dg(xb_scr[...], wgu_ref[...], NN)
    hidden = gu[:, 0:D_EXPERT]
    hidden = hidden * _sigmoid(hidden) * gu[:, D_EXPERT:2 * D_EXPERT]
    lane = _iota(gate_scr.shape, 1)
    ge = jnp.sum(jnp.where(lane == e, gate_scr[...], 0.0), axis=1, keepdims=True)
    acc_scr[...] += _dg((hidden * ge).astype(BF16), wd_ref[...], NN)

    @pl.when(e == pl.num_programs(1) - 1)
    def _():
        o_ref[...] = _layer_norm_rows(DEEPNORM_ALPHA * x_ref[...] + acc_scr[...], ln_ref[0:1, :], ln_ref[1:2, :])


def _moe(x2d, lw, shared, tm):
    n, dm = x2d.shape
    row = pl.BlockSpec((tm, dm), lambda i, e: (i, 0))
    full = lambda a: pl.BlockSpec(a.shape, lambda i, e: (0, 0))
    return pl.pallas_call(
        _moe_kernel,
        grid=(n // tm, N_EXPERTS),
        in_specs=[row, full(shared['wr_h']), full(shared['wr_l']), full(shared['router_bias']),
                  pl.BlockSpec((None, dm, 2 * D_EXPERT), lambda i, e: (e, 0, 0)),
                  pl.BlockSpec((None, D_EXPERT, dm), lambda i, e: (e, 0, 0)), full(lw['ln2'])],
        out_specs=row,
        out_shape=jax.ShapeDtypeStruct((n, dm), F32),
        scratch_shapes=[pltpu.VMEM((tm, dm), BF16), pltpu.VMEM((tm, LANES), F32), pltpu.VMEM((tm, dm), F32)],
        compiler_params=_cparams(("parallel", "arbitrary")),
        name="moe",
    )(x2d, shared['wr_h'], shared['wr_l'], shared['router_bias'], lw['moe_wgu'], lw['moe_wd'], lw['ln2'])


def _rope_tables(pos, head_dim, rot_dim):
    half = rot_dim // 2
    inv_freq = ROPE_THETA ** (-jnp.arange(half, dtype=F32) / half)
    ang = pos.astype(F32)[:, None] * inv_freq[None, :]
    cos, sin = jnp.cos(ang), jnp.sin(ang)
    lane = jnp.arange(LANES)
    within = lane % head_dim
    idx = within % half
    first = within < half
    second = (within >= half) & (within < rot_dim)
    c = jnp.where((first | second)[None, :], cos[:, idx], 1.0)
    sm = jnp.where(first[None, :], -sin[:, idx], 0.0)
    sp = jnp.where(second[None, :], sin[:, idx], 0.0)
    return jnp.concatenate([c, sm, sp], axis=1)


def _block_diag_heads(w):
    h, d, _ = w.shape
    eye = jnp.eye(h, dtype=w.dtype)
    return (eye[:, None, :, None] * w[:, :, None, :]).reshape(h * d, h * d)


def _row_pad(v, width=LANES):
    return jnp.pad(v, (0, width - v.shape[0]))[None, :]


def _prep_layer(l, p):
    w_in = p['w_in'][l]
    ia = 3 * W_MIX
    wi_cols = jnp.pad(w_in[:, ia:A_COLS], ((0, 0), (0, 2 * LANES - (A_COLS - ia))))
    wih = wi_cols.astype(BF16)
    wil = (wi_cols - wih.astype(F32)).astype(BF16)
    b0 = A_COLS
    c0 = b0 + B_PROJ
    d0 = c0 + 3 * W_MIX
    w6 = jnp.zeros((LANES, 3 * W_MIX), F32)
    w6 = w6.at[0:DECAY_LORA, 0:W_MIX].set(p['rwkv_w2'][l])
    w6 = w6.at[DECAY_LORA:DECAY_LORA + AAA_LORA, W_MIX:2 * W_MIX].set(p['rwkv_a2'][l])
    w6 = w6.at[DECAY_LORA + AAA_LORA:LANES, 2 * W_MIX:3 * W_MIX].set(p['rwkv_g2'][l])
    zeros = jnp.zeros((W_MIX,), F32)
    return {
        'wa': w_in[:, 0:ia].astype(BF16), 'wih': wih, 'wil': wil,
        'wb': w_in[:, b0:c0].astype(BF16), 'wc': w_in[:, c0:d0].astype(BF16), 'wd': w_in[:, d0:].astype(BF16),
        'idx_ln': jnp.concatenate([_row_pad(p['idx_ln_g'][l]), _row_pad(p['idx_ln_b'][l])], axis=0),
        'rwkv_mu': p['rwkv_mu'][l][None, :],
        'w6': w6.astype(BF16),
        'rwkv_vec': jnp.stack([p['rwkv_w0'][l], p['rwkv_a0'][l], p['rwkv_kk'][l], p['rwkv_ka'][l],
                               p['rwkv_lnx_g'][l], p['rwkv_lnx_b'][l], p['rwkv_rk'][l].reshape(-1), zeros]),
        'convc_w': p['convc_w'][l], 'convd_w': p['convd_w'][l],
        'lru_gw': jnp.concatenate([_block_diag_heads(p['lru_gx_w'][l]), _block_diag_heads(p['lru_ga_w'][l])],
                                  axis=1).astype(BF16),
        'lru_vec': jnp.stack([p['lru_gx_b'][l], p['lru_ga_b'][l], p['lru_lam'][l], p['convd_b'][l]]),
        'w_out': p['w_out'][l].astype(BF16),
        'ln1': jnp.stack([p['ln1_g'][l], p['ln1_b'][l]]),
        'ln2': jnp.stack([p['ln2_g'][l], p['ln2_b'][l]]),
        'moe_wgu': jnp.concatenate([p['moe_w_gate'][l], p['moe_w_up'][l]], axis=-1).astype(BF16),
        'moe_wd': p['moe_w_down'][l].astype(BF16),
    }


def _prompt_layer(x2d, lw, shared, tabs, b, t):
    q, k, v, qi, kiw, kirep, pb, pc, pd = _inproj(x2d, lw, tabs, t, 256)
    ya = _attn_prompt(q, qi, kiw, k, v, kirep, b, t)
    arrs = _rwkv_prep_prompt(pb, lw, t, 256)
    yb, rwkv_state = _rwkv_scan_prompt(arrs, lw, b, t, 2)
    yc, yd, bufc, bufd, h = _cd_prompt(pc, pd, lw, b, t, min(512, t))
    x1 = _outproj(x2d, (ya, yb, yc, yd), lw, 512)
    x2 = _moe(x1, lw, shared, min(1024, b * t))
    st = (k.reshape(b, t, A_HEADS, A_HEAD_DIM), v.reshape(b, t, A_HEADS, A_HEAD_DIM),
          kiw[:, 0:IDX_DIM].reshape(b, t, IDX_DIM), rwkv_state,
          pb.reshape(b, t, B_PROJ)[:, t - 1:t], bufc, bufd, h.reshape(b, W_MIX))
    return x2, st


def _sample_layer(l, x2d, lw, shared, tabs, bd, ts, caches, page_table, states):
    n = bd * ts
    rwkv_state, shift, conv_c, conv_d, lru_h = states
    q, k, v, qi, kiw, _, pb, pc, pd = _inproj(x2d, lw, tabs, n, 256)
    bm = lambda a: a.reshape(ts, bd, a.shape[-1]).transpose(1, 0, 2)
    ya = _attn_sample(l, bm(q), bm(qi), bm(kiw), bm(k), bm(v), page_table, *caches)
    ya = ya.transpose(1, 0, 2).reshape(n, W_MIX)
    arrs = _rwkv_prep_sample(pb, shift.reshape(bd, B_PROJ), lw, bd)
    yb, new_rwkv = _rwkv_scan_sample(arrs, rwkv_state, lw, bd, ts, 8)
    yc, yd, bufc, bufd, h = _cd_sample(pc.reshape(ts, bd, 3 * W_MIX), pd.reshape(ts, bd, 2 * W_MIX),
                                       conv_c.transpose(1, 0, 2), conv_d.transpose(1, 0, 2), lru_h, lw, bd, ts)
    x1 = _outproj(x2d, (ya, yb, yc.reshape(n, W_MIX), yd.reshape(n, W_MIX)), lw, 256)
    x2 = _moe(x1, lw, shared, n)
    kb, vb, kib = bm(k), bm(v), bm(kiw)
    st = (kb.reshape(bd, ts, A_HEADS, A_HEAD_DIM), vb.reshape(bd, ts, A_HEADS, A_HEAD_DIM), kib[:, :, 0:IDX_DIM],
          new_rwkv, pb.reshape(ts, bd, B_PROJ)[ts - 1][:, None, :], bufc.transpose(1, 0, 2),
          bufd.transpose(1, 0, 2), h)
    return x2, st


def kernel(x_prompt, x_sample, cache_k, cache_v, cache_ik, page_table, state_rwkv, state_rwkv_shift, state_conv_c, state_conv_d, state_lru, w_in, w_out, idx_ln_g, idx_ln_b, rwkv_mu, rwkv_w0, rwkv_w2, rwkv_a0, rwkv_a2, rwkv_g2, rwkv_kk, rwkv_ka, rwkv_rk, rwkv_lnx_g, rwkv_lnx_b, convc_w, convd_w, convd_b, lru_gx_w, lru_gx_b, lru_ga_w, lru_ga_b, lru_lam, ln1_g, ln1_b, ln2_g, ln2_b, w_router, router_bias, moe_w_gate, moe_w_up, moe_w_down):
    p = dict(w_in=w_in, w_out=w_out, idx_ln_g=idx_ln_g, idx_ln_b=idx_ln_b, rwkv_mu=rwkv_mu, rwkv_w0=rwkv_w0,
             rwkv_w2=rwkv_w2, rwkv_a0=rwkv_a0, rwkv_a2=rwkv_a2, rwkv_g2=rwkv_g2, rwkv_kk=rwkv_kk, rwkv_ka=rwkv_ka,
             rwkv_rk=rwkv_rk, rwkv_lnx_g=rwkv_lnx_g, rwkv_lnx_b=rwkv_lnx_b, convc_w=convc_w, convd_w=convd_w,
             convd_b=convd_b, lru_gx_w=lru_gx_w, lru_gx_b=lru_gx_b, lru_ga_w=lru_ga_w, lru_ga_b=lru_ga_b,
             lru_lam=lru_lam, ln1_g=ln1_g, ln1_b=ln1_b, ln2_g=ln2_g, ln2_b=ln2_b, moe_w_gate=moe_w_gate,
             moe_w_up=moe_w_up, moe_w_down=moe_w_down)
    depth = w_in.shape[0]
    b, t, dm = x_prompt.shape
    bd, ts, _ = x_sample.shape
    past_len = page_table.shape[1] * PAGE_SIZE

    wr = jnp.pad(w_router, ((0, 0), (0, LANES - N_EXPERTS)))
    wr_h = wr.astype(BF16)
    shared = {'wr_h': wr_h, 'wr_l': (wr - wr_h.astype(F32)).astype(BF16), 'router_bias': router_bias[:, None]}

    pos_p = jnp.arange(t)
    pos_s = jnp.repeat(past_len + jnp.arange(ts), bd)
    tabs_p = (_rope_tables(pos_p, A_HEAD_DIM, ROT_DIM), _rope_tables(pos_p, IDX_DIM, IDX_ROT_DIM),
              _rope_tables(pos_p, LANES, IDX_ROT_DIM))
    tabs_s = (_rope_tables(pos_s, A_HEAD_DIM, ROT_DIM), _rope_tables(pos_s, IDX_DIM, IDX_ROT_DIM),
              _rope_tables(pos_s, LANES, IDX_ROT_DIM))

    xp = x_prompt.reshape(b * t, dm)
    xs = x_sample.transpose(1, 0, 2).reshape(ts * bd, dm)
    new_p = [[] for _ in range(8)]
    new_s = [[] for _ in range(8)]
    for l in range(depth):
        lw = _prep_layer(l, p)
        xp, st_p = _prompt_layer(xp, lw, shared, tabs_p, b, t)
        xs, st_s = _sample_layer(l, xs, lw, shared, tabs_s, bd, ts, (cache_k, cache_v, cache_ik), page_table,
                                 (state_rwkv[l], state_rwkv_shift[l], state_conv_c[l], state_conv_d[l],
                                  state_lru[l]))
        for i in range(8):
            new_p[i].append(st_p[i])
            new_s[i].append(st_s[i])
    y_p = xp.reshape(b, t, dm)
    y_s = xs.reshape(ts, bd, dm).transpose(1, 0, 2)
    return (y_p, y_s) + tuple(jnp.stack(a) for a in new_p) + tuple(jnp.stack(a) for a in new_s)
```

```python
import functools
import math

import jax
import jax.numpy as jnp
from jax import lax
from jax.experimental import pallas as pl
from jax.experimental.pallas import tpu as pltpu

F32 = jnp.float32
BF16 = jnp.bfloat16
I32 = jnp.int32

A_HEADS = 4
A_HEAD_DIM = 64
ROT_DIM = 16
IDX_HEADS = 4
IDX_DIM = 32
IDX_ROT_DIM = 8
ROPE_THETA = 500000.0
TOPK_MAX = 256
PAGE_SIZE = 128
W_MIX = 256
B_HEAD = 64
B_HEADS = 4
DECAY_LORA = 32
AAA_LORA = 32
GATE_LORA = 64
B_PROJ = 3 * W_MIX + DECAY_LORA + AAA_LORA + GATE_LORA
LNX_EPS = 64e-5
C_CONV = 3
D_CONV = 4
D_HEADS = 4
LRU_C = 8.0
N_EXPERTS = 16
N_GROUPS = 4
EXPERTS_PER_GROUP = 4
D_EXPERT = 256
LN_EPS = 1e-5
DEPTH = 4
DEEPNORM_ALPHA = (2 * DEPTH) ** 0.25

A_COLS = 3 * W_MIX + IDX_HEADS * IDX_DIM + IDX_DIM + IDX_HEADS
A_PAD = 1024

LANES = 128
SUBLANES = 8
VMEM_LIMIT = 56 * 1024 * 1024

INT_MIN = -2147483648
NEG_BIG = -1e30
RWKV_CHUNK = 128


def _cparams(sem):
    return pltpu.CompilerParams(dimension_semantics=sem, vmem_limit_bytes=VMEM_LIMIT)


def _dg(a, b, dims):
    return lax.dot_general(a, b, (dims, ((), ())), preferred_element_type=F32)


NN = ((1,), (0,))
NT = ((1,), (1,))


def _mm(a, b, dims=NN):
    return _dg(a.astype(BF16), b.astype(BF16), dims)


def _split2(x):
    hi = x.astype(BF16)
    lo = (x - hi.astype(F32)).astype(BF16)
    return hi, lo


def _split3(x):
    hi = x.astype(BF16)
    r1 = x - hi.astype(F32)
    mid = r1.astype(BF16)
    lo = (r1 - mid.astype(F32)).astype(BF16)
    return hi, mid, lo


def _mm3(a, b, dims=NN):
    ah, al = _split2(a)
    bh, bl = _split2(b)
    return _dg(ah, bh, dims) + (_dg(ah, bl, dims) + _dg(al, bh, dims))


def _mm_xl(a, b_exact, dims=NN):
    h, m, l = _split3(a)
    bb = b_exact.astype(BF16)
    return _dg(h, bb, dims) + (_dg(m, bb, dims) + _dg(l, bb, dims))


def _mm_xr(a_exact, b, dims=NN):
    h, m, l = _split3(b)
    aa = a_exact.astype(BF16)
    return _dg(aa, h, dims) + (_dg(aa, m, dims) + _dg(aa, l, dims))


def _iota(shape, dim):
    return lax.broadcasted_iota(I32, shape, dim)


def _head_block_ones(n, blk):
    r = _iota((n, n), 0) // blk
    c = _iota((n, n), 1) // blk
    return jnp.where(r == c, 1.0, 0.0).astype(F32)


def _layer_norm_rows(x, g, b):
    mu = jnp.mean(x, axis=-1, keepdims=True)
    d = x - mu
    var = jnp.mean(d * d, axis=-1, keepdims=True)
    return d * lax.rsqrt(var + LN_EPS) * g + b


def _softplus(x):
    return jnp.maximum(x, 0.0) + jnp.log1p(jnp.exp(-jnp.abs(x)))


def _sigmoid(x):
    return 1.0 / (1.0 + jnp.exp(-x))


def _gelu_tanh(x):
    return 0.5 * x * (1.0 + jnp.tanh(math.sqrt(2.0 / math.pi) * (x + 0.044715 * (x * x * x))))


def _rope_block(blk, tab, half):
    c = tab[:, 0:LANES]
    sm = tab[:, LANES:2 * LANES]
    sp = tab[:, 2 * LANES:3 * LANES]
    return blk * c + pltpu.roll(blk, LANES - half, 1) * sm + pltpu.roll(blk, half, 1) * sp


def _inproj_kernel(x_ref, wa_ref, wih_ref, wil_ref, wb_ref, wc_ref, wd_ref, tqk_ref, tqi_ref, tki_ref, ln_ref,
                   pq_ref, pk_ref,
                   q_ref, k_ref, v_ref, qi_ref, kiw_ref, pb_ref, pc_ref, pd_ref,
                   qat_ref, kbf_ref, vbf_ref, qi3_ref, ki3_ref):
    x = x_ref[...]
    xh = x.astype(BF16)
    xl = (x - xh.astype(F32)).astype(BF16)
    acc = _dg(xh, wa_ref[...], NN)
    tqk = tqk_ref[...]
    for j in range(2):
        qj = _rope_block(acc[:, j * LANES:(j + 1) * LANES], tqk, ROT_DIM // 2)
        kj = _rope_block(acc[:, (2 + j) * LANES:(3 + j) * LANES], tqk, ROT_DIM // 2)
        q_ref[:, j * LANES:(j + 1) * LANES] = qj
        k_ref[:, j * LANES:(j + 1) * LANES] = kj
        qat_ref[:, j * LANES:(j + 1) * LANES] = (qj * (A_HEAD_DIM ** -0.5)).astype(BF16)
        kbf_ref[:, j * LANES:(j + 1) * LANES] = kj.astype(BF16)
    v = acc[:, 4 * LANES:6 * LANES]
    v_ref[...] = v
    vbf_ref[...] = v.astype(BF16)
    wih = wih_ref[...]
    acci = _dg(xh, wih, NN) + (_dg(xh, wil_ref[...], NN) + _dg(xl, wih, NN))
    qi = _rope_block(acci[:, 0:LANES], tqi_ref[...], IDX_ROT_DIM // 2)
    qi_ref[...] = qi
    qih, qil = _split2(qi)
    qi3_ref[...] = _dg(jnp.concatenate([qih, qil], axis=1), pq_ref[...], NN).astype(BF16)
    kw = acci[:, LANES:2 * LANES]
    lane = _iota(kw.shape, 1)
    in_ki = lane < IDX_DIM
    mu = jnp.sum(jnp.where(in_ki, kw, 0.0), axis=-1, keepdims=True) * (1.0 / IDX_DIM)
    d = jnp.where(in_ki, kw - mu, 0.0)
    var = jnp.sum(d * d, axis=-1, keepdims=True) * (1.0 / IDX_DIM)
    kn = d * lax.rsqrt(var + LN_EPS) * ln_ref[0:1, :] + ln_ref[1:2, :]
    kr = _rope_block(kn, tki_ref[...], IDX_ROT_DIM // 2)
    kr = jnp.where(in_ki, kr, 0.0)
    kiw_ref[...] = jnp.where(in_ki, kr, jnp.where(lane < IDX_DIM + IDX_HEADS, kw * (IDX_HEADS ** -0.5), 0.0))
    krh, krl = _split2(kr)
    ki3_ref[...] = _dg(jnp.concatenate([krh, krl], axis=1), pk_ref[...], NN).astype(BF16)
    pb_ref[...] = _dg(xh, wb_ref[...], NN)
    pc_ref[...] = _dg(xh, wc_ref[...], NN)
    pd_ref[...] = _dg(xh, wd_ref[...], NN)


def _placement_matrices():
    s = jnp.arange(2 * LANES)
    is_lo = s >= LANES
    h, d = (s % LANES) // IDX_DIM, s % IDX_DIM
    tq = jnp.arange(IDX_HEADS * LANES)
    q_hi = (tq[None, :] == (LANES * h + d)[:, None]) | (tq[None, :] == (LANES * h + 2 * IDX_DIM + d)[:, None])
    q_lo = tq[None, :] == (LANES * h + IDX_DIM + d)[:, None]
    pq = jnp.where(is_lo[:, None], q_lo, q_hi)
    tk = jnp.arange(LANES)
    src_ok = ((s % LANES) < IDX_DIM)[:, None]
    k_hi = (tk[None, :] == d[:, None]) | (tk[None, :] == (IDX_DIM + d)[:, None])
    k_lo = tk[None, :] == (2 * IDX_DIM + d)[:, None]
    pk = jnp.where(is_lo[:, None], k_lo, k_hi) & src_ok
    return pq.astype(BF16), pk.astype(BF16)


def _inproj(x2d, lw, tabs, tab_rows, tm):
    n, dm = x2d.shape
    nt = tab_rows // tm
    pq, pk = _placement_matrices()
    full = lambda a: pl.BlockSpec(a.shape, lambda i: (0, 0))
    row = lambda w: pl.BlockSpec((tm, w), lambda i: (i, 0))
    tab = lambda: pl.BlockSpec((tm, 3 * LANES), lambda i: (i % nt, 0))
    outs = [(W_MIX, F32), (W_MIX, F32), (W_MIX, F32), (LANES, F32), (LANES, F32), (B_PROJ, F32), (3 * W_MIX, F32),
            (2 * W_MIX, F32), (W_MIX, BF16), (W_MIX, BF16), (W_MIX, BF16), (IDX_HEADS * LANES, BF16), (LANES, BF16)]
    return pl.pallas_call(
        _inproj_kernel,
        grid=(n // tm,),
        in_specs=[row(dm), full(lw['wa']), full(lw['wih']), full(lw['wil']), full(lw['wb']), full(lw['wc']),
                  full(lw['wd']), tab(), tab(), tab(), full(lw['idx_ln']), full(pq), full(pk)],
        out_specs=[row(w) for w, _ in outs],
        out_shape=[jax.ShapeDtypeStruct((n, w), dt) for w, dt in outs],
        compiler_params=_cparams(("parallel",)),
        name="inproj",
    )(x2d, lw['wa'], lw['wih'], lw['wil'], lw['wb'], lw['wc'], lw['wd'], tabs[0], tabs[1], tabs[2], lw['idx_ln'],
      pq, pk)


def _float_order_key(score):
    score = jnp.where(score == 0.0, 0.0, score)
    bits = pltpu.bitcast(score, I32)
    return jnp.where(bits < 0, bits ^ jnp.int32(0x7FFFFFFF), bits)


def _kth_largest_key(count_ge, topk, shape, two_bits=False):
    kf = jnp.float32(topk)
    thr0 = jnp.where(count_ge(jnp.zeros(shape, I32)) >= kf, jnp.int32(0), jnp.int32(INT_MIN))
    if not two_bits:
        def body(i, thr):
            cand = thr + jnp.left_shift(jnp.int32(1), jnp.int32(30) - i)
            return jnp.where(count_ge(cand) >= kf, cand, thr)

        return lax.fori_loop(0, 31, body, thr0)

    cand = thr0 + jnp.int32(1 << 30)
    thr1 = jnp.where(count_ge(cand) >= kf, cand, thr0)

    def body2(i, thr):
        unit = jnp.left_shift(jnp.int32(1), jnp.int32(28) - 2 * i)
        steps = jnp.zeros(shape, I32)
        for mult in (1, 2, 3):
            steps = steps + jnp.where(count_ge(thr + mult * unit) >= kf, 1, 0)
        return thr + steps * unit

    return lax.fori_loop(0, 15, body2, thr1)


def _upper_tri_ones(n):
    return jnp.where(_iota((n, n), 0) <= _iota((n, n), 1), 1.0, 0.0).astype(BF16)


def _tie_select(key, thr, adm, need, off, tri):
    eq = jnp.where(key == thr, jnp.where(adm, 1.0, 0.0), 0.0)
    pre = _dg(eq.astype(BF16), tri, NN)
    take_tie = jnp.where((pre + off) <= need, eq, 0.0)
    sel = jnp.where(key > thr, 1.0, take_tie)
    return sel, off + pre[:, pre.shape[1] - 1:pre.shape[1]]


ATT_KC = 256


def _attn_prompt_kernel(topk, qat_ref, qi3_ref, kiwq_ref, k_ref, v_ref, ki3_ref, o_ref, key_scr, lg_scr):
    qb = qat_ref.shape[0]
    kc = ATT_KC
    j = pl.program_id(1)
    n_chunks = (j * qb + qb + kc - 1) // kc
    krow = _iota((kc, qb), 0)
    qpos = j * qb + _iota((kc, qb), 1)
    qst = jnp.concatenate([qi3_ref[:, h * LANES:(h + 1) * LANES] for h in range(IDX_HEADS)], axis=0)
    kiw_t = kiwq_ref[...].T
    wi = [kiw_t[IDX_DIM + h:IDX_DIM + h + 1, :] * (IDX_DIM ** -0.5) for h in range(IDX_HEADS)]

    def score_chunk(c, carry):
        base = pl.multiple_of(c * kc, kc)
        dots = _dg(ki3_ref[pl.ds(base, kc), :], qst, NT)
        score = wi[0] * jnp.maximum(dots[:, 0:qb], 0.0)
        for h in range(1, IDX_HEADS):
            score = score + wi[h] * jnp.maximum(dots[:, h * qb:(h + 1) * qb], 0.0)
        key_scr[c] = jnp.where(base + krow <= qpos, _float_order_key(score), jnp.int32(INT_MIN))
        return carry

    lax.fori_loop(0, n_chunks, score_chunk, 0)

    acc_rows = 4 * SUBLANES

    def count(pred):
        def body(c, acc):
            hit = jnp.where(pred(key_scr[c]), 1.0, 0.0)
            return acc + jnp.sum(hit.reshape(kc // acc_rows, acc_rows, qb), axis=0)

        acc = lax.fori_loop(0, n_chunks, body, jnp.zeros((acc_rows, qb), F32))
        return jnp.sum(acc, axis=0, keepdims=True)

    thr = _kth_largest_key(lambda cand: count(lambda k: k >= cand), topk, (1, qb))
    need = jnp.float32(topk) - count(lambda k: k > thr)

    lane_q = _iota((1, A_HEADS * A_HEAD_DIM), 1)
    q = qat_ref[...]
    qm = jnp.concatenate([jnp.where(lane_q // A_HEAD_DIM == h, q, jnp.zeros_like(q)) for h in range(A_HEADS)],
                         axis=0)
    tril = jnp.where(_iota((kc, kc), 0) >= _iota((kc, kc), 1), 1.0, 0.0).astype(BF16)
    rows4 = A_HEADS * qb

    def logits_chunk(c, carry):
        off, m_acc = carry
        base = pl.multiple_of(c * kc, kc)
        key = key_scr[c]
        eq = jnp.where(key == thr, jnp.where(base + krow <= qpos, 1.0, 0.0), 0.0)
        pre = _dg(tril, eq.astype(BF16), NN)
        take_tie = jnp.where((pre + off) <= need, eq, 0.0)
        sel = jnp.where(key > thr, 1.0, take_tie)
        bias = jnp.where(sel > 0.5, 0.0, NEG_BIG).T
        lg = _dg(qm, k_ref[pl.ds(base, kc), :], NT) + jnp.concatenate([bias] * A_HEADS, axis=0)
        lg_scr[c] = lg
        return off + pre[kc - 1:kc, :], jnp.maximum(m_acc, lg)

    _, m_acc = lax.fori_loop(0, n_chunks, logits_chunk,
                             (jnp.zeros((1, qb), F32), jnp.full((rows4, kc), NEG_BIG, F32)))
    m = jnp.max(m_acc, axis=1, keepdims=True)

    def value_chunk(c, carry):
        l_acc, acc = carry
        base = pl.multiple_of(c * kc, kc)
        p = jnp.exp(lg_scr[c] - m)
        return l_acc + p, acc + _dg(p.astype(BF16), v_ref[pl.ds(base, kc), :], NN)

    l_acc, acc = lax.fori_loop(0, n_chunks, value_chunk,
                               (jnp.zeros((rows4, kc), F32), jnp.zeros((rows4, A_HEADS * A_HEAD_DIM), F32)))
    acc = acc / jnp.sum(l_acc, axis=1, keepdims=True)
    out = jnp.zeros((qb, A_HEADS * A_HEAD_DIM), F32)
    for h in range(A_HEADS):
        out = out + jnp.where(lane_q // A_HEAD_DIM == h, acc[h * qb:(h + 1) * qb], 0.0)
    o_ref[...] = out


def _attn_prompt(qat, qi3, kiw, kbf, vbf, ki3, b, t):
    qb = LANES
    nq = t // qb
    topk = min(TOPK_MAX, t // 4)
    qspec = lambda w: pl.BlockSpec((qb, w), lambda bi, j: (bi * nq + j, 0))
    kspec = lambda w: pl.BlockSpec((t, w), lambda bi, j: (bi, 0))
    return pl.pallas_call(
        functools.partial(_attn_prompt_kernel, topk),
        grid=(b, nq),
        in_specs=[qspec(W_MIX), qspec(IDX_HEADS * LANES), qspec(LANES), kspec(W_MIX), kspec(W_MIX), kspec(LANES)],
        out_specs=qspec(W_MIX),
        out_shape=jax.ShapeDtypeStruct((b * t, W_MIX), F32),
        scratch_shapes=[pltpu.VMEM((t // ATT_KC, ATT_KC, qb), I32),
                        pltpu.VMEM((t // ATT_KC, A_HEADS * qb, ATT_KC), F32)],
        compiler_params=_cparams(("parallel", "arbitrary")),
        name="attn_prompt",
    )(qat, qi3, kiw, kbf, vbf, ki3)


SAMPLE_GB = 2


def _attn_sample_kernel(topk, n_pages, t_new, pt_ref, qm_ref, qst_ref, wst_ref, knew_ref, vnew_ref, kinew_ref,
                        *rest):
    gb = SAMPLE_GB
    n_refs = gb * n_pages
    k_pages, v_pages, ik_pages = rest[0:n_refs], rest[n_refs:2 * n_refs], rest[2 * n_refs:3 * n_refs]
    o_ref, key_scr, bias_scr, lg_scr = rest[3 * n_refs:]
    del pt_ref
    rows = SUBLANES
    n_chunks = n_pages + 1
    pad_rows = LANES - SUBLANES

    def chunk_t(pages, new_ref, g, c):
        if c < n_pages:
            return pages[g * n_pages + c][...]
        new = new_ref[g]
        return jnp.concatenate([new, jnp.zeros((pad_rows, new.shape[1]), F32)], axis=0).T

    def adm_of_chunk(c, nrows):
        if c < n_pages:
            return jnp.full((nrows, LANES), True)
        t = _iota((nrows, LANES), 0) % rows
        return _iota((nrows, LANES), 1) <= jnp.minimum(t, t_new - 1)

    for g in range(gb):
        qsh, qsl = _split2(qst_ref[g])
        wst = wst_ref[g]
        for c in range(n_chunks):
            ki = chunk_t(ik_pages, kinew_ref, g, c)[0:IDX_DIM]
            kih, kil = _split2(ki)
            dots = _dg(qsh, kih, NN) + (_dg(qsh, kil, NN) + _dg(qsl, kih, NN))
            sc = wst * jnp.maximum(dots * (IDX_DIM ** -0.5), 0.0)
            score = (sc[0:rows] + sc[rows:2 * rows]) + (sc[2 * rows:3 * rows] + sc[3 * rows:4 * rows])
            key_scr[g * rows:(g + 1) * rows, c * LANES:(c + 1) * LANES] = jnp.where(
                adm_of_chunk(c, rows), _float_order_key(score), jnp.int32(INT_MIN))

    all_rows = gb * rows
    count_ge = lambda cand: jnp.sum(jnp.where(key_scr[...] >= cand, 1.0, 0.0), axis=1, keepdims=True)
    thr = _kth_largest_key(count_ge, topk, (all_rows, 1), two_bits=True)
    need = jnp.float32(topk) - jnp.sum(jnp.where(key_scr[...] > thr, 1.0, 0.0), axis=1, keepdims=True)
    tri = _upper_tri_ones(LANES)
    off = jnp.zeros((all_rows, 1), F32)
    for c in range(n_chunks):
        sel, off = _tie_select(key_scr[:, c * LANES:(c + 1) * LANES], thr, adm_of_chunk(c, all_rows), need, off, tri)
        bias_scr[:, c * LANES:(c + 1) * LANES] = jnp.where(sel > 0.5, 0.0, NEG_BIG)

    lane_q = _iota((1, A_HEADS * A_HEAD_DIM), 1)
    hrows = A_HEADS * rows
    for g in range(gb):
        qm = qm_ref[g].astype(BF16)
        for c in range(n_chunks):
            kt = chunk_t(k_pages, knew_ref, g, c).astype(BF16)
            bias = bias_scr[g * rows:(g + 1) * rows, c * LANES:(c + 1) * LANES]
            lg_scr[g * hrows:(g + 1) * hrows, c * LANES:(c + 1) * LANES] = (
                _dg(qm, kt, NN) * (A_HEAD_DIM ** -0.5) + jnp.concatenate([bias] * A_HEADS, axis=0))
    lg = lg_scr[...]
    m = jnp.max(lg, axis=1, keepdims=True)
    lg_scr[...] = jnp.exp(lg - m)
    den = jnp.sum(lg_scr[...], axis=1, keepdims=True)
    for g in range(gb):
        acc = jnp.zeros((hrows, A_HEADS * A_HEAD_DIM), F32)
        for c in range(n_chunks):
            vt = chunk_t(v_pages, vnew_ref, g, c).astype(BF16)
            acc = acc + _dg(lg_scr[g * hrows:(g + 1) * hrows, c * LANES:(c + 1) * LANES].astype(BF16), vt, NT)
        acc = acc / den[g * hrows:(g + 1) * hrows]
        out = jnp.zeros((rows, A_HEADS * A_HEAD_DIM), F32)
        for h in range(A_HEADS):
            out = out + jnp.where(lane_q // A_HEAD_DIM == h, acc[h * rows:(h + 1) * rows], 0.0)
        o_ref[g] = out


def _attn_sample(layer, q, qi, kiw, k_new, v_new, page_table, cache_kt, cache_vt, cache_ikt):
    bd, ts, _ = q.shape
    gb = SAMPLE_GB
    n_pages = page_table.shape[1]
    rows = SUBLANES
    topk = min(TOPK_MAX, (n_pages * PAGE_SIZE + ts) // 4)
    padt = lambda a: jnp.pad(a, ((0, 0), (0, rows - ts), (0, 0)))
    qp = padt(q)
    head_of_lane = jnp.arange(A_HEADS * A_HEAD_DIM) // A_HEAD_DIM
    qm = jnp.where(head_of_lane[None, None, None, :] == jnp.arange(A_HEADS)[None, :, None, None],
                   qp[:, None, :, :], 0.0).reshape(bd, A_HEADS * rows, A_HEADS * A_HEAD_DIM)
    qst = padt(qi).reshape(bd, rows, IDX_HEADS, IDX_DIM).transpose(0, 2, 1, 3).reshape(bd, IDX_HEADS * rows, IDX_DIM)
    wi = padt(kiw[:, :, IDX_DIM:IDX_DIM + IDX_HEADS])
    wst = jnp.broadcast_to(wi.transpose(0, 2, 1).reshape(bd, IDX_HEADS * rows, 1), (bd, IDX_HEADS * rows, LANES))
    kp, vp, kinew = padt(k_new), padt(v_new), padt(kiw)
    per_b = lambda r, w: pl.BlockSpec((gb, r, w), lambda bi, pt: (bi, 0, 0))

    def page_spec(g, p, feat):
        return pl.BlockSpec((None, None, feat, PAGE_SIZE),
                            lambda bi, pt: (layer, pt[(bi * gb + g) * n_pages + p], 0, 0))

    in_specs = [per_b(A_HEADS * rows, W_MIX), per_b(IDX_HEADS * rows, IDX_DIM), per_b(IDX_HEADS * rows, LANES),
                per_b(rows, W_MIX), per_b(rows, W_MIX), per_b(rows, LANES)]
    for feat in (W_MIX, W_MIX, IDX_DIM):
        in_specs += [page_spec(g, p, feat) for g in range(gb) for p in range(n_pages)]
    width = (n_pages + 1) * LANES
    n_refs = gb * n_pages
    out = pl.pallas_call(
        functools.partial(_attn_sample_kernel, topk, n_pages, ts),
        grid_spec=pltpu.PrefetchScalarGridSpec(
            num_scalar_prefetch=1,
            grid=(bd // gb,),
            in_specs=in_specs,
            out_specs=pl.BlockSpec((gb, rows, W_MIX), lambda bi, pt: (bi, 0, 0)),
            scratch_shapes=[pltpu.VMEM((gb * rows, width), I32), pltpu.VMEM((gb * rows, width), F32),
                            pltpu.VMEM((gb * A_HEADS * rows, width), F32)]),
        out_shape=jax.ShapeDtypeStruct((bd, rows, W_MIX), F32),
        compiler_params=_cparams(("arbitrary",)),
        name="attn_sample",
    )(page_table.reshape(-1), qm, qst, wst, kp, vp, kinew,
      *([cache_kt] * n_refs), *([cache_vt] * n_refs), *([cache_ikt] * n_refs))
    return out[:, :ts]


def _rwkv_prep_math(p, prev, mu, w6, vecs, r_ref, lw_ref, k_ref, v_ref, a_ref, b_ref, g_ref):
    z = p + (prev - p) * mu
    r = z[:, 0:W_MIX]
    k = z[:, W_MIX:2 * W_MIX]
    v = z[:, 2 * W_MIX:3 * W_MIX]
    x6 = z[:, 3 * W_MIX:B_PROJ]
    lane = _iota(x6.shape, 1)
    act = jnp.where(lane < DECAY_LORA, jnp.tanh(x6),
                    jnp.where(lane < DECAY_LORA + AAA_LORA, x6, _sigmoid(x6)))
    lo = _mm(act, w6)
    w0, a0, kkw, ka = vecs[0:1], vecs[1:2], vecs[2:3], vecs[3:4]
    w_log = -_softplus(-(w0 + lo[:, 0:W_MIX])) - 0.5
    asig = _sigmoid(a0 + lo[:, W_MIX:2 * W_MIX])
    kk = k * kkw
    ss = _mm_xl(kk * kk, _head_block_ones(W_MIX, B_HEAD))
    kk = kk / jnp.maximum(jnp.sqrt(ss), 1e-12)
    r_ref[...] = r
    lw_ref[...] = -jnp.exp(w_log)
    k_ref[...] = k * (1.0 + (asig - 1.0) * ka)
    v_ref[...] = v
    a_ref[...] = -kk
    b_ref[...] = kk * asig
    g_ref[...] = lo[:, 2 * W_MIX:3 * W_MIX]


def _rwkv_prep_prompt_kernel(tiles_per_seq, p_ref, pprev_ref, mu_ref, w6_ref, vec_ref, *rest):
    outs, scr = rest[:7], rest[7]
    tm = p_ref.shape[0]
    p = p_ref[...]
    first = jnp.where(pl.program_id(0) % tiles_per_seq == 0, 0.0, pprev_ref[SUBLANES - 1:SUBLANES, :])
    scr[SUBLANES - 1:SUBLANES, :] = first
    scr[SUBLANES:SUBLANES + tm, :] = p
    prev = scr[SUBLANES - 1:SUBLANES - 1 + tm, :]
    _rwkv_prep_math(p, prev, mu_ref[...], w6_ref[...], vec_ref[...], *outs)


def _rwkv_prep_sample_kernel(bd, p_ref, shift_ref, mu_ref, w6_ref, vec_ref, *outs):
    p = p_ref[...]
    prev = jnp.concatenate([shift_ref[...], p[0:p.shape[0] - bd]], axis=0)
    _rwkv_prep_math(p, prev, mu_ref[...], w6_ref[...], vec_ref[...], *outs)


def _rwkv_prep_prompt(pb, lw, t, tm):
    n = pb.shape[0]
    tps = t // tm
    full = lambda a: pl.BlockSpec(a.shape, lambda i: (0, 0))
    row = lambda w: pl.BlockSpec((tm, w), lambda i: (i, 0))
    prev_spec = pl.BlockSpec((SUBLANES, B_PROJ), lambda i: (jnp.maximum(i * (tm // SUBLANES) - 1, 0), 0))
    return pl.pallas_call(
        functools.partial(_rwkv_prep_prompt_kernel, tps),
        grid=(n // tm,),
        in_specs=[row(B_PROJ), prev_spec, full(lw['rwkv_mu']), full(lw['w6']), full(lw['rwkv_vec'])],
        out_specs=[row(W_MIX)] * 7,
        out_shape=[jax.ShapeDtypeStruct((n, W_MIX), F32)] * 7,
        scratch_shapes=[pltpu.VMEM((tm + SUBLANES, B_PROJ), F32)],
        compiler_params=_cparams(("parallel",)),
        name="rwkv_prep_prompt",
    )(pb, pb, lw['rwkv_mu'], lw['w6'], lw['rwkv_vec'])


def _rwkv_prep_sample(pb, shift, lw, bd):
    n = pb.shape[0]
    full = lambda a: pl.BlockSpec(a.shape, lambda i: (0, 0))
    return pl.pallas_call(
        functools.partial(_rwkv_prep_sample_kernel, bd),
        grid=(1,),
        in_specs=[full(pb), full(shift), full(lw['rwkv_mu']), full(lw['w6']), full(lw['rwkv_vec'])],
        out_specs=[pl.BlockSpec((n, W_MIX), lambda i: (0, 0))] * 7,
        out_shape=[jax.ShapeDtypeStruct((n, W_MIX), F32)] * 7,
        compiler_params=_cparams(("arbitrary",)),
        name="rwkv_prep_sample",
    )(pb, shift, lw['rwkv_mu'], lw['w6'], lw['rwkv_vec'])


def _rwkv_post(y, r, k, v, g, vecs, jhead):
    lnx_g, lnx_b, rk = vecs[4:5], vecs[5:6], vecs[6:7]
    mu = _mm_xl(y, jhead) * (1.0 / B_HEAD)
    d = y - mu
    var = _mm_xl(d * d, jhead) * (1.0 / B_HEAD)
    yn = d * lax.rsqrt(var + LNX_EPS) * lnx_g + lnx_b
    bonus = _mm_xl(r * k * rk, jhead) * v
    return (yn + bonus) * g


def _rwkv_chunk_kernel(gb, r_ref, lw_ref, k_ref, v_ref, a_ref, b_ref, g_ref, vec_ref, y_ref, sfin_ref, s_scr):
    c = r_ref.shape[1]
    w = W_MIX
    ci = pl.program_id(1)

    @pl.when(ci == 0)
    def _():
        s_scr[...] = jnp.zeros(s_scr.shape, F32)

    row_c = _iota((c, c), 0)
    col_c = _iota((c, c), 1)
    incl = row_c >= col_c
    strict = row_c > col_c
    ltri = jnp.where(incl, 1.0, 0.0).astype(BF16)
    eye_c = jnp.where(row_c == col_c, 1.0, 0.0).astype(F32)
    jhead = _head_block_ones(w, B_HEAD)
    bm = jhead > 0.5
    eye_w = _iota((w, w), 0) == _iota((w, w), 1)
    lane = _iota((1, w), 1)
    vecs = vec_ref[...]
    levels = int(math.log2(c))

    masks = [lane // B_HEAD == h for h in range(B_HEADS)]
    pairs = [(bi, h) for bi in range(gb) for h in range(B_HEADS)]
    seq = []
    for bi in range(gb):
        r, lw, k, v, a, b = r_ref[bi], lw_ref[bi], k_ref[bi], v_ref[bi], a_ref[bi], b_ref[bi]
        cum = _mm_xr(ltri, lw)
        last = cum[c - 1:c, :]
        p_inv = jnp.exp(-cum)
        p_end = jnp.exp(last - cum)
        seq.append(dict(r=r, k=k, v=v, last=last, at=a * jnp.exp(cum - lw), rt=r * jnp.exp(cum),
                        bt=(b * p_inv).astype(BF16), kt=(k * p_inv).astype(BF16),
                        bpt=(b * p_end).T.astype(BF16), kpt=(k * p_end).T.astype(BF16), vb=v.astype(BF16)))
    n_mat, aak, brb, brk = {}, {}, {}, {}
    for bi, h in pairs:
        s = seq[bi]
        ar = jnp.concatenate([jnp.where(masks[h], s['at'], 0.0), jnp.where(masks[h], s['rt'], 0.0)],
                             axis=0).astype(BF16)
        mb = _dg(ar, s['bt'], NT)
        mk = _dg(ar, s['kt'], NT)
        n_mat[bi, h] = jnp.where(strict, mb[0:c], 0.0)
        aak[bi, h] = jnp.where(strict, mk[0:c], 0.0).astype(BF16)
        brb[bi, h] = jnp.where(incl, mb[c:2 * c], 0.0).astype(BF16)
        brk[bi, h] = jnp.where(incl, mk[c:2 * c], 0.0).astype(BF16)
    tinv = {p: eye_c + n_mat[p] for p in pairs}
    npow = n_mat
    for _ in range(levels - 1):
        npow = {p: _mm(npow[p], npow[p]) for p in pairs}
        tinv = {p: tinv[p] + _mm(tinv[p], npow[p]) for p in pairs}
    tinv = {p: tinv[p].astype(BF16) for p in pairs}
    x = {(bi, h): _dg(aak[bi, h], seq[bi]['vb'], NN).astype(BF16) for bi, h in pairs}
    for bi in range(gb):
        s = seq[bi]
        atb = s['at'].astype(BF16)
        wt = jnp.zeros((c, w), F32)
        u0 = jnp.zeros((c, w), F32)
        for h in range(B_HEADS):
            wt = wt + jnp.where(masks[h], _dg(tinv[bi, h], atb, NN), 0.0)
            u0 = u0 + jnp.where(masks[h], _dg(tinv[bi, h], x[bi, h], NN), 0.0)
        s['wtb'] = wt.astype(BF16)
        s['u0b'] = u0.astype(BF16)
    for bi in range(gb):
        s = seq[bi]
        ry = s['rt']
        y0 = jnp.zeros((c, w), F32)
        for h in range(B_HEADS):
            ry = ry + jnp.where(masks[h], _dg(brb[bi, h], s['wtb'], NN), 0.0)
            y0 = y0 + jnp.where(masks[h], _dg(brb[bi, h], s['u0b'], NN) + _dg(brk[bi, h], s['vb'], NN), 0.0)
        gt = jnp.where(eye_w, jnp.exp(s['last']), 0.0) + jnp.where(bm, _dg(s['bpt'], s['wtb'], NN), 0.0)
        ht = jnp.where(bm, _dg(s['bpt'], s['u0b'], NN) + _dg(s['kpt'], s['vb'], NN), 0.0)
        s0 = s_scr[bi]
        y = _mm3(ry, s0) + y0
        s_scr[bi] = _mm3(gt, s0) + ht
        y_ref[bi] = _rwkv_post(y, s['r'], s['k'], s['v'], g_ref[bi], vecs, jhead)

    @pl.when(ci == pl.num_programs(1) - 1)
    def _():
        fold = jnp.where(_iota((w, B_HEAD), 0) % B_HEAD == _iota((w, B_HEAD), 1), 1.0, 0.0)
        for bi in range(gb):
            sfin_ref[bi] = _mm_xl(s_scr[bi], fold)


def _rwkv_scan_prompt(arrs, lw, b, t, gb):
    c = RWKV_CHUNK
    a3 = [x.reshape(b, t, W_MIX) for x in arrs]
    blk = pl.BlockSpec((gb, c, W_MIX), lambda i, j: (i, j, 0))
    y, sfin = pl.pallas_call(
        functools.partial(_rwkv_chunk_kernel, gb),
        grid=(b // gb, t // c),
        in_specs=[blk] * 7 + [pl.BlockSpec(lw['rwkv_vec'].shape, lambda i, j: (0, 0))],
        out_specs=[blk, pl.BlockSpec((gb, W_MIX, B_HEAD), lambda i, j: (i, 0, 0))],
        out_shape=[jax.ShapeDtypeStruct((b, t, W_MIX), F32), jax.ShapeDtypeStruct((b, W_MIX, B_HEAD), F32)],
        scratch_shapes=[pltpu.VMEM((gb, W_MIX, W_MIX), F32)],
        compiler_params=_cparams(("parallel", "arbitrary")),
        name="rwkv_chunk",
    )(*a3, lw['rwkv_vec'])
    state = sfin.reshape(b, B_HEADS, B_HEAD, B_HEAD).transpose(0, 1, 3, 2)
    return y.reshape(b * t, W_MIX), state


def _rwkv_step_kernel(gb, r_ref, lw_ref, k_ref, v_ref, a_ref, b_ref, g_ref, s_ref, vec_ref, y_ref, so_ref):
    ts = r_ref.shape[0]
    w = W_MIX
    jhead = _head_block_ones(w, B_HEAD)
    q1 = jnp.where(_iota((B_HEAD, w), 0) == _iota((B_HEAD, w), 1) % B_HEAD, 1.0, 0.0)
    q1g = jnp.concatenate([q1] * gb, axis=0)
    vecs = vec_ref[...]

    def rows(x):
        return jnp.concatenate([jnp.broadcast_to(x[bi:bi + 1, :], (B_HEAD, w)) for bi in range(gb)], axis=0)

    s = s_ref[...].reshape(gb * B_HEAD, w)
    for t in range(ts):
        r_t, k_t, v_t = r_ref[t], k_ref[t], v_ref[t]
        sa = _mm_xl(s * rows(a_ref[t]), jhead)
        vcol = _mm_xl(q1g * rows(v_t), jhead)
        s = s * rows(jnp.exp(lw_ref[t])) + sa * rows(b_ref[t]) + vcol * rows(k_t)
        yb = _mm_xl(s * rows(r_t), jhead)
        y_t = jnp.sum((yb * q1g).reshape(gb, B_HEAD, w), axis=1)
        y_ref[t] = _rwkv_post(y_t, r_t, k_t, v_t, g_ref[t], vecs, jhead)
    so_ref[...] = s.reshape(gb, B_HEAD, w)


def _rwkv_scan_sample(arrs, state, lw, bd, ts, gb):
    a3 = [x.reshape(ts, bd, W_MIX) for x in arrs]
    s_in = state.transpose(0, 2, 1, 3).reshape(bd, B_HEAD, W_MIX)
    blk = pl.BlockSpec((ts, gb, W_MIX), lambda i: (0, i, 0))
    sblk = pl.BlockSpec((gb, B_HEAD, W_MIX), lambda i: (i, 0, 0))
    y, s_out = pl.pallas_call(
        functools.partial(_rwkv_step_kernel, gb),
        grid=(bd // gb,),
        in_specs=[blk] * 7 + [sblk, pl.BlockSpec(lw['rwkv_vec'].shape, lambda i: (0, 0))],
        out_specs=[blk, sblk],
        out_shape=[jax.ShapeDtypeStruct((ts, bd, W_MIX), F32), jax.ShapeDtypeStruct((bd, B_HEAD, W_MIX), F32)],
        compiler_params=_cparams(("parallel",)),
        name="rwkv_step",
    )(*a3, s_in, lw['rwkv_vec'])
    new_state = s_out.reshape(bd, B_HEAD, B_HEADS, B_HEAD).transpose(0, 2, 1, 3)
    return y.reshape(ts * bd, W_MIX), new_state


def _lru_inputs(xc, gate_w_ref, vec):
    gates = _mm(xc, gate_w_ref[...])
    gate_x = _sigmoid(gates[:, 0:W_MIX] + vec[0:1])
    gate_a = _sigmoid(gates[:, W_MIX:2 * W_MIX] + vec[1:2])
    log_a = -LRU_C * gate_a * _softplus(-vec[2:3])
    a2 = jnp.exp(2.0 * log_a)
    neg_expm1 = -jnp.tanh(log_a) * (a2 + 1.0)
    u = jnp.sqrt(neg_expm1) * gate_x * xc
    return jnp.exp(log_a), u


def _cd_prompt_kernel(pc_ref, pd_ref, cw_ref, dw_ref, gw_ref, vec_ref,
                      yc_ref, yd_ref, bufc_ref, bufd_ref, h_ref, gx_scr, xb_scr, a_scr, u_scr, hs_scr, h_scr):
    tt = pc_ref.shape[0]
    ti = pl.program_id(1)
    pad = SUBLANES

    @pl.when(ti == 0)
    def _():
        gx_scr[0:pad, :] = jnp.zeros((pad, W_MIX), F32)
        xb_scr[0:pad, :] = jnp.zeros((pad, W_MIX), F32)
        h_scr[...] = jnp.zeros(h_scr.shape, F32)

    @pl.when(ti > 0)
    def _():
        gx_scr[0:pad, :] = gx_scr[tt:tt + pad, :]
        xb_scr[0:pad, :] = xb_scr[tt:tt + pad, :]

    vec = vec_ref[...]
    gx_scr[pad:pad + tt, :] = pc_ref[:, W_MIX:2 * W_MIX] * pc_ref[:, 2 * W_MIX:3 * W_MIX]
    conv = cw_ref[0:1, :] * gx_scr[pad - 2:pad - 2 + tt, :]
    for j in range(1, C_CONV):
        conv = conv + cw_ref[j:j + 1, :] * gx_scr[pad - 2 + j:pad - 2 + j + tt, :]
    yc_ref[...] = pc_ref[:, 0:W_MIX] * conv
    bufc_ref[...] = gx_scr[pad + tt - (C_CONV - 1):pad + tt, :]
    xb_scr[pad:pad + tt, :] = pd_ref[:, 0:W_MIX]
    xc = dw_ref[0:1, :] * xb_scr[pad - 3:pad - 3 + tt, :]
    for j in range(1, D_CONV):
        xc = xc + dw_ref[j:j + 1, :] * xb_scr[pad - 3 + j:pad - 3 + j + tt, :]
    xc = xc + vec[3:4]
    bufd_ref[...] = xb_scr[pad + tt - (D_CONV - 1):pad + tt, :]
    a, u = _lru_inputs(xc, gw_ref, vec)
    a_scr[...] = a
    u_scr[...] = u
    row8 = _iota((SUBLANES, W_MIX), 0)

    def group(gi, h):
        base = pl.multiple_of(gi * SUBLANES, SUBLANES)
        a8 = a_scr[pl.ds(base, SUBLANES), :]
        u8 = u_scr[pl.ds(base, SUBLANES), :]
        hs8 = jnp.zeros((SUBLANES, W_MIX), F32)
        for j in range(SUBLANES):
            h = a8[j:j + 1, :] * h + u8[j:j + 1, :]
            hs8 = jnp.where(row8 == j, h, hs8)
        hs_scr[pl.ds(base, SUBLANES), :] = hs8
        return h

    h = lax.fori_loop(0, tt // SUBLANES, group, h_scr[...])
    h_scr[...] = h
    h_ref[...] = h
    yd_ref[...] = hs_scr[...] * _gelu_tanh(pd_ref[:, W_MIX:2 * W_MIX])


def _cd_prompt(pc, pd, lw, b, t, tt):
    nt = t // tt
    row = lambda w: pl.BlockSpec((tt, w), lambda bi, ti: (bi * nt + ti, 0))
    full = lambda a: pl.BlockSpec(a.shape, lambda bi, ti: (0, 0))
    per_b = lambda r: pl.BlockSpec((None, r, W_MIX), lambda bi, ti: (bi, 0, 0))
    big = lambda: pltpu.VMEM((tt + 2 * SUBLANES, W_MIX), F32)
    tile = lambda: pltpu.VMEM((tt, W_MIX), F32)
    return pl.pallas_call(
        _cd_prompt_kernel,
        grid=(b, nt),
        in_specs=[row(3 * W_MIX), row(2 * W_MIX), full(lw['convc_w']), full(lw['convd_w']), full(lw['lru_gw']),
                  full(lw['lru_vec'])],
        out_specs=[row(W_MIX), row(W_MIX), per_b(C_CONV - 1), per_b(D_CONV - 1), per_b(1)],
        out_shape=[jax.ShapeDtypeStruct((b * t, W_MIX), F32), jax.ShapeDtypeStruct((b * t, W_MIX), F32),
                   jax.ShapeDtypeStruct((b, C_CONV - 1, W_MIX), F32),
                   jax.ShapeDtypeStruct((b, D_CONV - 1, W_MIX), F32), jax.ShapeDtypeStruct((b, 1, W_MIX), F32)],
        scratch_shapes=[big(), big(), tile(), tile(), tile(), pltpu.VMEM((1, W_MIX), F32)],
        compiler_params=_cparams(("parallel", "arbitrary")),
        name="cd_prompt",
    )(pc, pd, lw['convc_w'], lw['convd_w'], lw['lru_gw'], lw['lru_vec'])


def _cd_sample_kernel(pc_ref, pd_ref, bc_ref, bd_ref, h0_ref, cw_ref, dw_ref, gw_ref, vec_ref,
                      yc_ref, yd_ref, bufc_ref, bufd_ref, h_ref):
    ts = pc_ref.shape[0]
    vec = vec_ref[...]
    gx = [bc_ref[j] for j in range(C_CONV - 1)]
    gx += [pc_ref[t][:, W_MIX:2 * W_MIX] * pc_ref[t][:, 2 * W_MIX:3 * W_MIX] for t in range(ts)]
    for t in range(ts):
        conv = cw_ref[0:1, :] * gx[t]
        for j in range(1, C_CONV):
            conv = conv + cw_ref[j:j + 1, :] * gx[t + j]
        yc_ref[t] = pc_ref[t][:, 0:W_MIX] * conv
    for j in range(C_CONV - 1):
        bufc_ref[j] = gx[ts + j]
    xb = [bd_ref[j] for j in range(D_CONV - 1)] + [pd_ref[t][:, 0:W_MIX] for t in range(ts)]
    h = h0_ref[...]
    for t in range(ts):
        xc = dw_ref[0:1, :] * xb[t]
        for j in range(1, D_CONV):
            xc = xc + dw_ref[j:j + 1, :] * xb[t + j]
        xc = xc + vec[3:4]
        a, u = _lru_inputs(xc, gw_ref, vec)
        h = a * h + u
        yd_ref[t] = h * _gelu_tanh(pd_ref[t][:, W_MIX:2 * W_MIX])
    for j in range(D_CONV - 1):
        bufd_ref[j] = xb[ts + j]
    h_ref[...] = h


def _cd_sample(pc, pd, buf_c, buf_d, h0, lw, bd, ts):
    args = (pc, pd, buf_c, buf_d, h0, lw['convc_w'], lw['convd_w'], lw['lru_gw'], lw['lru_vec'])
    full = lambda a: pl.BlockSpec(a.shape, lambda i: (0,) * a.ndim)
    shapes = [(ts, bd, W_MIX), (ts, bd, W_MIX), (C_CONV - 1, bd, W_MIX), (D_CONV - 1, bd, W_MIX), (bd, W_MIX)]
    return pl.pallas_call(
        _cd_sample_kernel,
        grid=(1,),
        in_specs=[full(a) for a in args],
        out_specs=[pl.BlockSpec(s, lambda i, n=len(s): (0,) * n) for s in shapes],
        out_shape=[jax.ShapeDtypeStruct(s, F32) for s in shapes],
        compiler_params=_cparams(("arbitrary",)),
        name="cd_sample",
    )(*args)


def _outproj_kernel(x_ref, ya_ref, yb_ref, yc_ref, yd_ref, w_ref, ln_ref, o_ref):
    mix = _dg(ya_ref[...].astype(BF16), w_ref[0:W_MIX, :], NN)
    mix = mix + _dg(yb_ref[...].astype(BF16), w_ref[W_MIX:2 * W_MIX, :], NN)
    mix = mix + _dg(yc_ref[...].astype(BF16), w_ref[2 * W_MIX:3 * W_MIX, :], NN)
    mix = mix + _dg(yd_ref[...].astype(BF16), w_ref[3 * W_MIX:4 * W_MIX, :], NN)
    o_ref[...] = _layer_norm_rows(DEEPNORM_ALPHA * x_ref[...] + mix, ln_ref[0:1, :], ln_ref[1:2, :])


def _outproj(x2d, ys, lw, tm):
    n, dm = x2d.shape
    row = lambda w: pl.BlockSpec((tm, w), lambda i: (i, 0))
    full = lambda a: pl.BlockSpec(a.shape, lambda i: (0, 0))
    return pl.pallas_call(
        _outproj_kernel,
        grid=(n // tm,),
        in_specs=[row(dm)] + [row(W_MIX)] * 4 + [full(lw['w_out']), full(lw['ln1'])],
        out_specs=row(dm),
        out_shape=jax.ShapeDtypeStruct((n, dm), F32),
        compiler_params=_cparams(("parallel",)),
        name="outproj",
    )(x2d, *ys, lw['w_out'], lw['ln1'])


def _route(x, wr_h, wr_l, bias_col):
    xh, xl = _split2(x)
    logits = _dg(xh, wr_h, NN) + (_dg(xh, wr_l, NN) + _dg(xl, wr_h, NN))
    st = _sigmoid(logits.T[0:N_EXPERTS, :])
    sel = st + bias_col
    rows = [sel[e:e + 1, :] for e in range(N_EXPERTS)]
    in_top2 = []
    for e in range(N_EXPERTS):
        g0 = (e // EXPERTS_PER_GROUP) * EXPERTS_PER_GROUP
        rank = jnp.zeros(rows[e].shape, F32)
        for o in range(g0, g0 + EXPERTS_PER_GROUP):
            if o == e:
                continue
            ahead = (rows[o] >= rows[e]) if o < e else (rows[o] > rows[e])
            rank = rank + jnp.where(ahead, 1.0, 0.0)
        in_top2.append(rank < float(2))
    gscore = []
    for g in range(N_GROUPS):
        acc = jnp.zeros(rows[0].shape, F32)
        for e in range(g * EXPERTS_PER_GROUP, (g + 1) * EXPERTS_PER_GROUP):
            acc = acc + jnp.where(in_top2[e], rows[e], 0.0)
        gscore.append(acc)
    best = gscore[0]
    gidx = jnp.zeros(best.shape, I32)
    for g in range(1, N_GROUPS):
        better = gscore[g] > best
        best = jnp.where(better, gscore[g], best)
        gidx = jnp.where(better, g, gidx)
    picked = [jnp.where(in_top2[e] & (gidx == e // EXPERTS_PER_GROUP), st[e:e + 1, :], 0.0)
              for e in range(N_EXPERTS)]
    total = picked[0]
    for e in range(1, N_EXPERTS):
        total = total + picked[e]
    return jnp.concatenate(picked, axis=0) / total


def _moe_kernel(x_ref, wrh_ref, wrl_ref, rb_ref, wgu_ref, wd_ref, ln_ref, o_ref, xb_scr, gate_scr, acc_scr):
    e = pl.program_id(1)

    @pl.when(e == 0)
    def _():
        x = x_ref[...]
        xb_scr[...] = x.astype(BF16)
        gates = _route(x, wrh_ref[...], wrl_ref[...], rb_ref[...])
        pad = jnp.zeros((LANES - N_EXPERTS, gates.shape[1]), F32)
        gate_scr[...] = jnp.concatenate([gates, pad], axis=0).T
        acc_scr[...] = jnp.zeros(acc_scr.shape, F32)

    gu = _dg(xb_scr[...], wgu_ref[...], NN)
    hidden = gu[:, 0:D_EXPERT]
    hidden = hidden * _sigmoid(hidden) * gu[:, D_EXPERT:2 * D_EXPERT]
    lane = _iota(gate_scr.shape, 1)
    ge = jnp.sum(jnp.where(lane == e, gate_scr[...], 0.0), axis=1, keepdims=True)
    acc_scr[...] += _dg((hidden * ge).astype(BF16), wd_ref[...], NN)

    @pl.when(e == pl.num_programs(1) - 1)
    def _():
        o_ref[...] = _layer_norm_rows(DEEPNORM_ALPHA * x_ref[...] + acc_scr[...], ln_ref[0:1, :], ln_ref[1:2, :])


def _moe(x2d, lw, shared, tm):
    n, dm = x2d.shape
    row = pl.BlockSpec((tm, dm), lambda i, e: (i, 0))
    full = lambda a: pl.BlockSpec(a.shape, lambda i, e: (0, 0))
    return pl.pallas_call(
        _moe_kernel,
        grid=(n // tm, N_EXPERTS),
        in_specs=[row, full(shared['wr_h']), full(shared['wr_l']), full(shared['router_bias']),
                  pl.BlockSpec((None, dm, 2 * D_EXPERT), lambda i, e: (e, 0, 0)),
                  pl.BlockSpec((None, D_EXPERT, dm), lambda i, e: (e, 0, 0)), full(lw['ln2'])],
        out_specs=row,
        out_shape=jax.ShapeDtypeStruct((n, dm), F32),
        scratch_shapes=[pltpu.VMEM((tm, dm), BF16), pltpu.VMEM((tm, LANES), F32), pltpu.VMEM((tm, dm), F32)],
        compiler_params=_cparams(("parallel", "arbitrary")),
        name="moe",
    )(x2d, shared['wr_h'], shared['wr_l'], shared['router_bias'], lw['moe_wgu'], lw['moe_wd'], lw['ln2'])


def _rope_tables(pos, head_dim, rot_dim):
    half = rot_dim // 2
    inv_freq = ROPE_THETA ** (-jnp.arange(half, dtype=F32) / half)
    ang = pos.astype(F32)[:, None] * inv_freq[None, :]
    cos, sin = jnp.cos(ang), jnp.sin(ang)
    lane = jnp.arange(LANES)
    within = lane % head_dim
    idx = within % half
    first = within < half
    second = (within >= half) & (within < rot_dim)
    c = jnp.where((first | second)[None, :], cos[:, idx], 1.0)
    sm = jnp.where(first[None, :], -sin[:, idx], 0.0)
    sp = jnp.where(second[None, :], sin[:, idx], 0.0)
    return jnp.concatenate([c, sm, sp], axis=1)


def _block_diag_heads(w):
    h, d, _ = w.shape
    eye = jnp.eye(h, dtype=w.dtype)
    return (eye[:, None, :, None] * w[:, :, None, :]).reshape(h * d, h * d)


def _row_pad(v, width=LANES):
    return jnp.pad(v, (0, width - v.shape[0]))[None, :]


def _prep_layer(l, p):
    w_in = p['w_in'][l]
    ia = 3 * W_MIX
    wi_cols = jnp.pad(w_in[:, ia:A_COLS], ((0, 0), (0, 2 * LANES - (A_COLS - ia))))
    wih = wi_cols.astype(BF16)
    wil = (wi_cols - wih.astype(F32)).astype(BF16)
    b0 = A_COLS
    c0 = b0 + B_PROJ
    d0 = c0 + 3 * W_MIX
    w6 = jnp.zeros((LANES, 3 * W_MIX), F32)
    w6 = w6.at[0:DECAY_LORA, 0:W_MIX].set(p['rwkv_w2'][l])
    w6 = w6.at[DECAY_LORA:DECAY_LORA + AAA_LORA, W_MIX:2 * W_MIX].set(p['rwkv_a2'][l])
    w6 = w6.at[DECAY_LORA + AAA_LORA:LANES, 2 * W_MIX:3 * W_MIX].set(p['rwkv_g2'][l])
    zeros = jnp.zeros((W_MIX,), F32)
    return {
        'wa': w_in[:, 0:ia].astype(BF16), 'wih': wih, 'wil': wil,
        'wb': w_in[:, b0:c0].astype(BF16), 'wc': w_in[:, c0:d0].astype(BF16), 'wd': w_in[:, d0:].astype(BF16),
        'idx_ln': jnp.concatenate([_row_pad(p['idx_ln_g'][l]), _row_pad(p['idx_ln_b'][l])], axis=0),
        'rwkv_mu': p['rwkv_mu'][l][None, :],
        'w6': w6.astype(BF16),
        'rwkv_vec': jnp.stack([p['rwkv_w0'][l], p['rwkv_a0'][l], p['rwkv_kk'][l], p['rwkv_ka'][l],
                               p['rwkv_lnx_g'][l], p['rwkv_lnx_b'][l], p['rwkv_rk'][l].reshape(-1), zeros]),
        'convc_w': p['convc_w'][l], 'convd_w': p['convd_w'][l],
        'lru_gw': jnp.concatenate([_block_diag_heads(p['lru_gx_w'][l]), _block_diag_heads(p['lru_ga_w'][l])],
                                  axis=1).astype(BF16),
        'lru_vec': jnp.stack([p['lru_gx_b'][l], p['lru_ga_b'][l], p['lru_lam'][l], p['convd_b'][l]]),
        'w_out': p['w_out'][l].astype(BF16),
        'ln1': jnp.stack([p['ln1_g'][l], p['ln1_b'][l]]),
        'ln2': jnp.stack([p['ln2_g'][l], p['ln2_b'][l]]),
        'moe_wgu': jnp.concatenate([p['moe_w_gate'][l], p['moe_w_up'][l]], axis=-1).astype(BF16),
        'moe_wd': p['moe_w_down'][l].astype(BF16),
    }


def _prompt_layer(x2d, lw, shared, tabs, b, t):
    _, k, v, _, kiw, pb, pc, pd, qat, kbf, vbf, qi3, ki3 = _inproj(x2d, lw, tabs, t, 256)
    ya = _attn_prompt(qat, qi3, kiw, kbf, vbf, ki3, b, t)
    arrs = _rwkv_prep_prompt(pb, lw, t, 256)
    yb, rwkv_state = _rwkv_scan_prompt(arrs, lw, b, t, min(4, b))
    yc, yd, bufc, bufd, h = _cd_prompt(pc, pd, lw, b, t, min(512, t))
    x1 = _outproj(x2d, (ya, yb, yc, yd), lw, 512)
    x2 = _moe(x1, lw, shared, min(1024, b * t))
    st = (k.reshape(b, t, A_HEADS, A_HEAD_DIM), v.reshape(b, t, A_HEADS, A_HEAD_DIM),
          kiw[:, 0:IDX_DIM].reshape(b, t, IDX_DIM), rwkv_state,
          pb.reshape(b, t, B_PROJ)[:, t - 1:t], bufc, bufd, h.reshape(b, W_MIX))
    return x2, st


def _sample_layer(l, x2d, lw, shared, tabs, bd, ts, caches, page_table, states):
    n = bd * ts
    rwkv_state, shift, conv_c, conv_d, lru_h = states
    q, k, v, qi, kiw, pb, pc, pd = _inproj(x2d, lw, tabs, n, 256)[0:8]
    bm = lambda a: a.reshape(ts, bd, a.shape[-1]).transpose(1, 0, 2)
    ya = _attn_sample(l, bm(q), bm(qi), bm(kiw), bm(k), bm(v), page_table, *caches)
    ya = ya.transpose(1, 0, 2).reshape(n, W_MIX)
    arrs = _rwkv_prep_sample(pb, shift.reshape(bd, B_PROJ), lw, bd)
    yb, new_rwkv = _rwkv_scan_sample(arrs, rwkv_state, lw, bd, ts, 16)
    yc, yd, bufc, bufd, h = _cd_sample(pc.reshape(ts, bd, 3 * W_MIX), pd.reshape(ts, bd, 2 * W_MIX),
                                       conv_c.transpose(1, 0, 2), conv_d.transpose(1, 0, 2), lru_h, lw, bd, ts)
    x1 = _outproj(x2d, (ya, yb, yc.reshape(n, W_MIX), yd.reshape(n, W_MIX)), lw, 256)
    x2 = _moe(x1, lw, shared, n)
    kb, vb, kib = bm(k), bm(v), bm(kiw)
    st = (kb.reshape(bd, ts, A_HEADS, A_HEAD_DIM), vb.reshape(bd, ts, A_HEADS, A_HEAD_DIM), kib[:, :, 0:IDX_DIM],
          new_rwkv, pb.reshape(ts, bd, B_PROJ)[ts - 1][:, None, :], bufc.transpose(1, 0, 2),
          bufd.transpose(1, 0, 2), h)
    return x2, st


def kernel(x_prompt, x_sample, cache_k, cache_v, cache_ik, page_table, state_rwkv, state_rwkv_shift, state_conv_c, state_conv_d, state_lru, w_in, w_out, idx_ln_g, idx_ln_b, rwkv_mu, rwkv_w0, rwkv_w2, rwkv_a0, rwkv_a2, rwkv_g2, rwkv_kk, rwkv_ka, rwkv_rk, rwkv_lnx_g, rwkv_lnx_b, convc_w, convd_w, convd_b, lru_gx_w, lru_gx_b, lru_ga_w, lru_ga_b, lru_lam, ln1_g, ln1_b, ln2_g, ln2_b, w_router, router_bias, moe_w_gate, moe_w_up, moe_w_down):
    p = dict(w_in=w_in, w_out=w_out, idx_ln_g=idx_ln_g, idx_ln_b=idx_ln_b, rwkv_mu=rwkv_mu, rwkv_w0=rwkv_w0,
             rwkv_w2=rwkv_w2, rwkv_a0=rwkv_a0, rwkv_a2=rwkv_a2, rwkv_g2=rwkv_g2, rwkv_kk=rwkv_kk, rwkv_ka=rwkv_ka,
             rwkv_rk=rwkv_rk, rwkv_lnx_g=rwkv_lnx_g, rwkv_lnx_b=rwkv_lnx_b, convc_w=convc_w, convd_w=convd_w,
             convd_b=convd_b, lru_gx_w=lru_gx_w, lru_gx_b=lru_gx_b, lru_ga_w=lru_ga_w, lru_ga_b=lru_ga_b,
             lru_lam=lru_lam, ln1_g=ln1_g, ln1_b=ln1_b, ln2_g=ln2_g, ln2_b=ln2_b, moe_w_gate=moe_w_gate,
             moe_w_up=moe_w_up, moe_w_down=moe_w_down)
    depth = w_in.shape[0]
    b, t, dm = x_prompt.shape
    bd, ts, _ = x_sample.shape
    past_len = page_table.shape[1] * PAGE_SIZE

    wr = jnp.pad(w_router, ((0, 0), (0, LANES - N_EXPERTS)))
    wr_h = wr.astype(BF16)
    shared = {'wr_h': wr_h, 'wr_l': (wr - wr_h.astype(F32)).astype(BF16), 'router_bias': router_bias[:, None]}

    pos_p = jnp.arange(t)
    pos_s = jnp.repeat(past_len + jnp.arange(ts), bd)
    tabs_p = (_rope_tables(pos_p, A_HEAD_DIM, ROT_DIM), _rope_tables(pos_p, IDX_DIM, IDX_ROT_DIM),
              _rope_tables(pos_p, LANES, IDX_ROT_DIM))
    tabs_s = (_rope_tables(pos_s, A_HEAD_DIM, ROT_DIM), _rope_tables(pos_s, IDX_DIM, IDX_ROT_DIM),
              _rope_tables(pos_s, LANES, IDX_ROT_DIM))

    n_phys = cache_k.shape[1]
    caches_t = (cache_k.transpose(0, 1, 3, 4, 2).reshape(depth, n_phys, W_MIX, PAGE_SIZE),
                cache_v.transpose(0, 1, 3, 4, 2).reshape(depth, n_phys, W_MIX, PAGE_SIZE),
                cache_ik.transpose(0, 1, 3, 2))
    xp = x_prompt.reshape(b * t, dm)
    xs = x_sample.transpose(1, 0, 2).reshape(ts * bd, dm)
    new_p = [[] for _ in range(8)]
    new_s = [[] for _ in range(8)]
    for l in range(depth):
        lw = _prep_layer(l, p)
        xp, st_p = _prompt_layer(xp, lw, shared, tabs_p, b, t)
        xs, st_s = _sample_layer(l, xs, lw, shared, tabs_s, bd, ts, caches_t, page_table,
                                 (state_rwkv[l], state_rwkv_shift[l], state_conv_c[l], state_conv_d[l],
                                  state_lru[l]))
        for i in range(8):
            new_p[i].append(st_p[i])
            new_s[i].append(st_s[i])
    y_p = xp.reshape(b, t, dm)
    y_s = xs.reshape(ts, bd, dm).transpose(1, 0, 2)
    return (y_p, y_s) + tuple(jnp.stack(a) for a in new_p) + tuple(jnp.stack(a) for a in new_s)
```

```python
import functools
import math

import jax
import jax.numpy as jnp
from jax import lax
from jax.experimental import pallas as pl
from jax.experimental.pallas import tpu as pltpu

F32 = jnp.float32
BF16 = jnp.bfloat16
I32 = jnp.int32

A_HEADS = 4
A_HEAD_DIM = 64
ROT_DIM = 16
IDX_HEADS = 4
IDX_DIM = 32
IDX_ROT_DIM = 8
ROPE_THETA = 500000.0
TOPK_MAX = 256
PAGE_SIZE = 128
W_MIX = 256
B_HEAD = 64
B_HEADS = 4
DECAY_LORA = 32
AAA_LORA = 32
GATE_LORA = 64
B_PROJ = 3 * W_MIX + DECAY_LORA + AAA_LORA + GATE_LORA
LNX_EPS = 64e-5
C_CONV = 3
D_CONV = 4
D_HEADS = 4
LRU_C = 8.0
N_EXPERTS = 16
N_GROUPS = 4
EXPERTS_PER_GROUP = 4
D_EXPERT = 256
LN_EPS = 1e-5
DEPTH = 4
DEEPNORM_ALPHA = (2 * DEPTH) ** 0.25

A_COLS = 3 * W_MIX + IDX_HEADS * IDX_DIM + IDX_DIM + IDX_HEADS
A_PAD = 1024

LANES = 128
SUBLANES = 8
VMEM_LIMIT = 56 * 1024 * 1024

INT_MIN = -2147483648
NEG_BIG = -1e30
RWKV_CHUNK = 128


def _cparams(sem):
    return pltpu.CompilerParams(dimension_semantics=sem, vmem_limit_bytes=VMEM_LIMIT)


def _dg(a, b, dims):
    return lax.dot_general(a, b, (dims, ((), ())), preferred_element_type=F32)


NN = ((1,), (0,))
NT = ((1,), (1,))


def _mm(a, b, dims=NN):
    return _dg(a.astype(BF16), b.astype(BF16), dims)


def _split2(x):
    hi = x.astype(BF16)
    lo = (x - hi.astype(F32)).astype(BF16)
    return hi, lo


def _split3(x):
    hi = x.astype(BF16)
    r1 = x - hi.astype(F32)
    mid = r1.astype(BF16)
    lo = (r1 - mid.astype(F32)).astype(BF16)
    return hi, mid, lo


def _mm3(a, b, dims=NN):
    ah, al = _split2(a)
    bh, bl = _split2(b)
    return _dg(ah, bh, dims) + (_dg(ah, bl, dims) + _dg(al, bh, dims))


def _mm_xl(a, b_exact, dims=NN):
    h, m, l = _split3(a)
    bb = b_exact.astype(BF16)
    return _dg(h, bb, dims) + (_dg(m, bb, dims) + _dg(l, bb, dims))


def _mm_xr(a_exact, b, dims=NN):
    h, m, l = _split3(b)
    aa = a_exact.astype(BF16)
    return _dg(aa, h, dims) + (_dg(aa, m, dims) + _dg(aa, l, dims))


def _iota(shape, dim):
    return lax.broadcasted_iota(I32, shape, dim)


def _head_block_ones(n, blk):
    r = _iota((n, n), 0) // blk
    c = _iota((n, n), 1) // blk
    return jnp.where(r == c, 1.0, 0.0).astype(F32)


def _layer_norm_rows(x, g, b):
    mu = jnp.mean(x, axis=-1, keepdims=True)
    d = x - mu
    var = jnp.mean(d * d, axis=-1, keepdims=True)
    return d * lax.rsqrt(var + LN_EPS) * g + b


def _softplus(x):
    return jnp.maximum(x, 0.0) + jnp.log1p(jnp.exp(-jnp.abs(x)))


def _sigmoid(x):
    return 1.0 / (1.0 + jnp.exp(-x))


def _gelu_tanh(x):
    return 0.5 * x * (1.0 + jnp.tanh(math.sqrt(2.0 / math.pi) * (x + 0.044715 * (x * x * x))))


def _rope_block(blk, tab, half):
    c = tab[:, 0:LANES]
    sm = tab[:, LANES:2 * LANES]
    sp = tab[:, 2 * LANES:3 * LANES]
    return blk * c + pltpu.roll(blk, LANES - half, 1) * sm + pltpu.roll(blk, half, 1) * sp


def _inproj_kernel(x_ref, wa_ref, wih_ref, wil_ref, wb_ref, wc_ref, wd_ref, tqk_ref, tqi_ref, tki_ref, ln_ref,
                   pq_ref, pk_ref,
                   q_ref, k_ref, v_ref, qi_ref, kiw_ref, pb_ref, pc_ref, pd_ref,
                   qat_ref, kbf_ref, vbf_ref, qi3_ref, ki3_ref):
    x = x_ref[...]
    xh = x.astype(BF16)
    xl = (x - xh.astype(F32)).astype(BF16)
    acc = _dg(xh, wa_ref[...], NN)
    tqk = tqk_ref[...]
    for j in range(2):
        qj = _rope_block(acc[:, j * LANES:(j + 1) * LANES], tqk, ROT_DIM // 2)
        kj = _rope_block(acc[:, (2 + j) * LANES:(3 + j) * LANES], tqk, ROT_DIM // 2)
        q_ref[:, j * LANES:(j + 1) * LANES] = qj
        k_ref[:, j * LANES:(j + 1) * LANES] = kj
        qat_ref[:, j * LANES:(j + 1) * LANES] = (qj * (A_HEAD_DIM ** -0.5)).astype(BF16)
        kbf_ref[:, j * LANES:(j + 1) * LANES] = kj.astype(BF16)
    v = acc[:, 4 * LANES:6 * LANES]
    v_ref[...] = v
    vbf_ref[...] = v.astype(BF16)
    wih = wih_ref[...]
    acci = _dg(xh, wih, NN) + (_dg(xh, wil_ref[...], NN) + _dg(xl, wih, NN))
    qi = _rope_block(acci[:, 0:LANES], tqi_ref[...], IDX_ROT_DIM // 2)
    qi_ref[...] = qi
    qih, qil = _split2(qi)
    qi3_ref[...] = _dg(jnp.concatenate([qih, qil], axis=1), pq_ref[...], NN).astype(BF16)
    kw = acci[:, LANES:2 * LANES]
    lane = _iota(kw.shape, 1)
    in_ki = lane < IDX_DIM
    mu = jnp.sum(jnp.where(in_ki, kw, 0.0), axis=-1, keepdims=True) * (1.0 / IDX_DIM)
    d = jnp.where(in_ki, kw - mu, 0.0)
    var = jnp.sum(d * d, axis=-1, keepdims=True) * (1.0 / IDX_DIM)
    kn = d * lax.rsqrt(var + LN_EPS) * ln_ref[0:1, :] + ln_ref[1:2, :]
    kr = _rope_block(kn, tki_ref[...], IDX_ROT_DIM // 2)
    kr = jnp.where(in_ki, kr, 0.0)
    kiw_ref[...] = jnp.where(in_ki, kr, jnp.where(lane < IDX_DIM + IDX_HEADS, kw * (IDX_HEADS ** -0.5), 0.0))
    krh, krl = _split2(kr)
    ki3_ref[...] = _dg(jnp.concatenate([krh, krl], axis=1), pk_ref[...], NN).astype(BF16)
    pb_ref[...] = _dg(xh, wb_ref[...], NN)
    pc_ref[...] = _dg(xh, wc_ref[...], NN)
    pd_ref[...] = _dg(xh, wd_ref[...], NN)


def _placement_matrices():
    s = jnp.arange(2 * LANES)
    is_lo = s >= LANES
    h, d = (s % LANES) // IDX_DIM, s % IDX_DIM
    tq = jnp.arange(IDX_HEADS * LANES)
    q_hi = (tq[None, :] == (LANES * h + d)[:, None]) | (tq[None, :] == (LANES * h + 2 * IDX_DIM + d)[:, None])
    q_lo = tq[None, :] == (LANES * h + IDX_DIM + d)[:, None]
    pq = jnp.where(is_lo[:, None], q_lo, q_hi)
    tk = jnp.arange(LANES)
    src_ok = ((s % LANES) < IDX_DIM)[:, None]
    k_hi = (tk[None, :] == d[:, None]) | (tk[None, :] == (IDX_DIM + d)[:, None])
    k_lo = tk[None, :] == (2 * IDX_DIM + d)[:, None]
    pk = jnp.where(is_lo[:, None], k_lo, k_hi) & src_ok
    return pq.astype(BF16), pk.astype(BF16)


def _inproj(x2d, lw, tabs, tab_rows, tm):
    n, dm = x2d.shape
    nt = tab_rows // tm
    pq, pk = _placement_matrices()
    full = lambda a: pl.BlockSpec(a.shape, lambda i: (0, 0))
    row = lambda w: pl.BlockSpec((tm, w), lambda i: (i, 0))
    tab = lambda: pl.BlockSpec((tm, 3 * LANES), lambda i: (i % nt, 0))
    outs = [(W_MIX, F32), (W_MIX, F32), (W_MIX, F32), (LANES, F32), (LANES, F32), (B_PROJ, F32), (3 * W_MIX, F32),
            (2 * W_MIX, F32), (W_MIX, BF16), (W_MIX, BF16), (W_MIX, BF16), (IDX_HEADS * LANES, BF16), (LANES, BF16)]
    return pl.pallas_call(
        _inproj_kernel,
        grid=(n // tm,),
        in_specs=[row(dm), full(lw['wa']), full(lw['wih']), full(lw['wil']), full(lw['wb']), full(lw['wc']),
                  full(lw['wd']), tab(), tab(), tab(), full(lw['idx_ln']), full(pq), full(pk)],
        out_specs=[row(w) for w, _ in outs],
        out_shape=[jax.ShapeDtypeStruct((n, w), dt) for w, dt in outs],
        compiler_params=_cparams(("parallel",)),
        name="inproj",
    )(x2d, lw['wa'], lw['wih'], lw['wil'], lw['wb'], lw['wc'], lw['wd'], tabs[0], tabs[1], tabs[2], lw['idx_ln'],
      pq, pk)


def _float_order_key(score):
    score = jnp.where(score == 0.0, 0.0, score)
    bits = pltpu.bitcast(score, I32)
    return jnp.where(bits < 0, bits ^ jnp.int32(0x7FFFFFFF), bits)


def _kth_largest_key(count_ge, topk, shape, two_bits=False):
    kf = jnp.float32(topk)
    thr0 = jnp.where(count_ge(jnp.zeros(shape, I32)) >= kf, jnp.int32(0), jnp.int32(INT_MIN))
    if not two_bits:
        def body(i, thr):
            cand = thr + jnp.left_shift(jnp.int32(1), jnp.int32(30) - i)
            return jnp.where(count_ge(cand) >= kf, cand, thr)

        return lax.fori_loop(0, 31, body, thr0)

    cand = thr0 + jnp.int32(1 << 30)
    thr1 = jnp.where(count_ge(cand) >= kf, cand, thr0)

    def body2(i, thr):
        unit = jnp.left_shift(jnp.int32(1), jnp.int32(28) - 2 * i)
        steps = jnp.zeros(shape, I32)
        for mult in (1, 2, 3):
            steps = steps + jnp.where(count_ge(thr + mult * unit) >= kf, 1, 0)
        return thr + steps * unit

    return lax.fori_loop(0, 15, body2, thr1)


def _upper_tri_ones(n):
    return jnp.where(_iota((n, n), 0) <= _iota((n, n), 1), 1.0, 0.0).astype(BF16)


def _tie_select(key, thr, adm, need, off, tri):
    eq = jnp.where(key == thr, jnp.where(adm, 1.0, 0.0), 0.0)
    pre = _dg(eq.astype(BF16), tri, NN)
    take_tie = jnp.where((pre + off) <= need, eq, 0.0)
    sel = jnp.where(key > thr, 1.0, take_tie)
    return sel, off + pre[:, pre.shape[1] - 1:pre.shape[1]]


ATT_KC = 512


def _attn_prompt_kernel(topk, qat_ref, qi3_ref, kiwq_ref, k_ref, v_ref, ki3_ref, o_ref, key_scr, lg_scr):
    qb = qat_ref.shape[0]
    kc = key_scr.shape[1]
    j = pl.program_id(1)
    n_chunks = (j * qb + qb + kc - 1) // kc
    krow = _iota((kc, qb), 0)
    qpos = j * qb + _iota((kc, qb), 1)
    qst = jnp.concatenate([qi3_ref[:, h * LANES:(h + 1) * LANES] for h in range(IDX_HEADS)], axis=0)
    kiw_t = kiwq_ref[...].T
    wi = [kiw_t[IDX_DIM + h:IDX_DIM + h + 1, :] * (IDX_DIM ** -0.5) for h in range(IDX_HEADS)]

    def score_chunk(c, carry):
        base = pl.multiple_of(c * kc, kc)
        dots = _dg(ki3_ref[pl.ds(base, kc), :], qst, NT)
        score = wi[0] * jnp.maximum(dots[:, 0:qb], 0.0)
        for h in range(1, IDX_HEADS):
            score = score + wi[h] * jnp.maximum(dots[:, h * qb:(h + 1) * qb], 0.0)
        key_scr[c] = jnp.where(base + krow <= qpos, _float_order_key(score), jnp.int32(INT_MIN))
        return carry

    lax.fori_loop(0, n_chunks, score_chunk, 0)

    acc_rows = 4 * SUBLANES

    def count(pred):
        def body(c, acc):
            hit = jnp.where(pred(key_scr[c]), 1.0, 0.0)
            return acc + jnp.sum(hit.reshape(kc // acc_rows, acc_rows, qb), axis=0)

        acc = lax.fori_loop(0, n_chunks, body, jnp.zeros((acc_rows, qb), F32))
        return jnp.sum(acc, axis=0, keepdims=True)

    thr = _kth_largest_key(lambda cand: count(lambda k: k >= cand), topk, (1, qb))
    need = jnp.float32(topk) - count(lambda k: k > thr)

    lane_q = _iota((1, A_HEADS * A_HEAD_DIM), 1)
    q = qat_ref[...]
    qm = jnp.concatenate([jnp.where(lane_q // A_HEAD_DIM == h, q, jnp.zeros_like(q)) for h in range(A_HEADS)],
                         axis=0)
    tril = jnp.where(_iota((kc, kc), 0) >= _iota((kc, kc), 1), 1.0, 0.0).astype(BF16)
    rows4 = A_HEADS * qb
    lane_blocks = lambda a: [a[:, i * LANES:(i + 1) * LANES] for i in range(kc // LANES)]

    def logits_chunk(c, carry):
        off, m = carry
        base = pl.multiple_of(c * kc, kc)
        key = key_scr[c]
        eq = jnp.where(key == thr, jnp.where(base + krow <= qpos, 1.0, 0.0), 0.0)
        pre = _dg(tril, eq.astype(BF16), NN)
        take_tie = jnp.where((pre + off) <= need, eq, 0.0)
        sel = jnp.where(key > thr, 1.0, take_tie)
        bias = jnp.where(sel > 0.5, 0.0, NEG_BIG).T
        lg = _dg(qm, k_ref[pl.ds(base, kc), :], NT) + jnp.concatenate([bias] * A_HEADS, axis=0)
        lg_scr[c] = lg
        return off + pre[kc - 1:kc, :], functools.reduce(jnp.maximum, lane_blocks(lg), m)

    _, m = lax.fori_loop(0, n_chunks, logits_chunk,
                         (jnp.zeros((1, qb), F32), jnp.full((rows4, LANES), NEG_BIG, F32)))
    m = jnp.max(m, axis=1, keepdims=True)

    def value_chunk(c, carry):
        l, acc = carry
        base = pl.multiple_of(c * kc, kc)
        p = jnp.exp(lg_scr[c] - m)
        return (functools.reduce(jnp.add, lane_blocks(p), l),
                acc + _dg(p.astype(BF16), v_ref[pl.ds(base, kc), :], NN))

    l, acc = lax.fori_loop(0, n_chunks, value_chunk,
                           (jnp.zeros((rows4, LANES), F32), jnp.zeros((rows4, A_HEADS * A_HEAD_DIM), F32)))
    acc = acc / jnp.sum(l, axis=1, keepdims=True)
    out = jnp.zeros((qb, A_HEADS * A_HEAD_DIM), F32)
    for h in range(A_HEADS):
        out = out + jnp.where(lane_q // A_HEAD_DIM == h, acc[h * qb:(h + 1) * qb], 0.0)
    o_ref[...] = out


def _attn_prompt(qat, qi3, kiw, kbf, vbf, ki3, b, t):
    qb = LANES
    nq = t // qb
    kc = min(ATT_KC, t)
    topk = min(TOPK_MAX, t // 4)
    qspec = lambda w: pl.BlockSpec((qb, w), lambda bi, j: (bi * nq + j, 0))
    kspec = lambda w: pl.BlockSpec((t, w), lambda bi, j: (bi, 0))
    return pl.pallas_call(
        functools.partial(_attn_prompt_kernel, topk),
        grid=(b, nq),
        in_specs=[qspec(W_MIX), qspec(IDX_HEADS * LANES), qspec(LANES), kspec(W_MIX), kspec(W_MIX), kspec(LANES)],
        out_specs=qspec(W_MIX),
        out_shape=jax.ShapeDtypeStruct((b * t, W_MIX), F32),
        scratch_shapes=[pltpu.VMEM((t // kc, kc, qb), I32), pltpu.VMEM((t // kc, A_HEADS * qb, kc), F32)],
        compiler_params=_cparams(("parallel", "arbitrary")),
        name="attn_prompt",
    )(qat, qi3, kiw, kbf, vbf, ki3)


SAMPLE_GB = 4


def _attn_sample_kernel(topk, n_pages, t_new, pt_ref, qm_ref, qst_ref, wst_ref, knew_ref, vnew_ref, kinew_ref,
                        *rest):
    gb = SAMPLE_GB
    n_refs = gb * n_pages
    k_pages, v_pages, ik_pages = rest[0:n_refs], rest[n_refs:2 * n_refs], rest[2 * n_refs:3 * n_refs]
    o_ref, key_scr, bias_scr, lg_scr = rest[3 * n_refs:]
    del pt_ref
    rows = SUBLANES
    n_chunks = n_pages + 1
    pad_rows = LANES - SUBLANES

    def chunk_t(pages, new_ref, g, c):
        if c < n_pages:
            return pages[g * n_pages + c][...]
        new = new_ref[g]
        return jnp.concatenate([new, jnp.zeros((pad_rows, new.shape[1]), F32)], axis=0).T

    def adm_of_chunk(c, nrows):
        if c < n_pages:
            return jnp.full((nrows, LANES), True)
        t = _iota((nrows, LANES), 0) % rows
        return _iota((nrows, LANES), 1) <= jnp.minimum(t, t_new - 1)

    for g in range(gb):
        qsh, qsl = _split2(qst_ref[g])
        wst = wst_ref[g]
        for c in range(n_chunks):
            ki = chunk_t(ik_pages, kinew_ref, g, c)[0:IDX_DIM]
            kih, kil = _split2(ki)
            dots = _dg(qsh, kih, NN) + (_dg(qsh, kil, NN) + _dg(qsl, kih, NN))
            sc = wst * jnp.maximum(dots * (IDX_DIM ** -0.5), 0.0)
            score = (sc[0:rows] + sc[rows:2 * rows]) + (sc[2 * rows:3 * rows] + sc[3 * rows:4 * rows])
            key_scr[g * rows:(g + 1) * rows, c * LANES:(c + 1) * LANES] = jnp.where(
                adm_of_chunk(c, rows), _float_order_key(score), jnp.int32(INT_MIN))

    all_rows = gb * rows
    count_ge = lambda cand: jnp.sum(jnp.where(key_scr[...] >= cand, 1.0, 0.0), axis=1, keepdims=True)
    thr = _kth_largest_key(count_ge, topk, (all_rows, 1), two_bits=True)
    need = jnp.float32(topk) - jnp.sum(jnp.where(key_scr[...] > thr, 1.0, 0.0), axis=1, keepdims=True)
    tri = _upper_tri_ones(LANES)
    off = jnp.zeros((all_rows, 1), F32)
    for c in range(n_chunks):
        sel, off = _tie_select(key_scr[:, c * LANES:(c + 1) * LANES], thr, adm_of_chunk(c, all_rows), need, off, tri)
        bias_scr[:, c * LANES:(c + 1) * LANES] = jnp.where(sel > 0.5, 0.0, NEG_BIG)

    lane_q = _iota((1, A_HEADS * A_HEAD_DIM), 1)
    hrows = A_HEADS * rows
    for g in range(gb):
        qm = qm_ref[g].astype(BF16)
        for c in range(n_chunks):
            kt = chunk_t(k_pages, knew_ref, g, c).astype(BF16)
            bias = bias_scr[g * rows:(g + 1) * rows, c * LANES:(c + 1) * LANES]
            lg_scr[g * hrows:(g + 1) * hrows, c * LANES:(c + 1) * LANES] = (
                _dg(qm, kt, NN) * (A_HEAD_DIM ** -0.5) + jnp.concatenate([bias] * A_HEADS, axis=0))
    lg = lg_scr[...]
    m = jnp.max(lg, axis=1, keepdims=True)
    lg_scr[...] = jnp.exp(lg - m)
    den = jnp.sum(lg_scr[...], axis=1, keepdims=True)
    for g in range(gb):
        acc = jnp.zeros((hrows, A_HEADS * A_HEAD_DIM), F32)
        for c in range(n_chunks):
            vt = chunk_t(v_pages, vnew_ref, g, c).astype(BF16)
            acc = acc + _dg(lg_scr[g * hrows:(g + 1) * hrows, c * LANES:(c + 1) * LANES].astype(BF16), vt, NT)
        acc = acc / den[g * hrows:(g + 1) * hrows]
        out = jnp.zeros((rows, A_HEADS * A_HEAD_DIM), F32)
        for h in range(A_HEADS):
            out = out + jnp.where(lane_q // A_HEAD_DIM == h, acc[h * rows:(h + 1) * rows], 0.0)
        o_ref[g] = out


def _attn_sample(layer, q, qi, kiw, k_new, v_new, page_table, cache_kt, cache_vt, cache_ikt):
    bd, ts, _ = q.shape
    gb = SAMPLE_GB
    n_pages = page_table.shape[1]
    rows = SUBLANES
    topk = min(TOPK_MAX, (n_pages * PAGE_SIZE + ts) // 4)
    padt = lambda a: jnp.pad(a, ((0, 0), (0, rows - ts), (0, 0)))
    qp = padt(q)
    head_of_lane = jnp.arange(A_HEADS * A_HEAD_DIM) // A_HEAD_DIM
    qm = jnp.where(head_of_lane[None, None, None, :] == jnp.arange(A_HEADS)[None, :, None, None],
                   qp[:, None, :, :], 0.0).reshape(bd, A_HEADS * rows, A_HEADS * A_HEAD_DIM)
    qst = padt(qi).reshape(bd, rows, IDX_HEADS, IDX_DIM).transpose(0, 2, 1, 3).reshape(bd, IDX_HEADS * rows, IDX_DIM)
    wi = padt(kiw[:, :, IDX_DIM:IDX_DIM + IDX_HEADS])
    wst = jnp.broadcast_to(wi.transpose(0, 2, 1).reshape(bd, IDX_HEADS * rows, 1), (bd, IDX_HEADS * rows, LANES))
    kp, vp, kinew = padt(k_new), padt(v_new), padt(kiw)
    per_b = lambda r, w: pl.BlockSpec((gb, r, w), lambda bi, pt: (bi, 0, 0))

    def page_spec(g, p, feat):
        return pl.BlockSpec((None, None, feat, PAGE_SIZE),
                            lambda bi, pt: (layer, pt[(bi * gb + g) * n_pages + p], 0, 0))

    in_specs = [per_b(A_HEADS * rows, W_MIX), per_b(IDX_HEADS * rows, IDX_DIM), per_b(IDX_HEADS * rows, LANES),
                per_b(rows, W_MIX), per_b(rows, W_MIX), per_b(rows, LANES)]
    for feat in (W_MIX, W_MIX, IDX_DIM):
        in_specs += [page_spec(g, p, feat) for g in range(gb) for p in range(n_pages)]
    width = (n_pages + 1) * LANES
    n_refs = gb * n_pages
    out = pl.pallas_call(
        functools.partial(_attn_sample_kernel, topk, n_pages, ts),
        grid_spec=pltpu.PrefetchScalarGridSpec(
            num_scalar_prefetch=1,
            grid=(bd // gb,),
            in_specs=in_specs,
            out_specs=pl.BlockSpec((gb, rows, W_MIX), lambda bi, pt: (bi, 0, 0)),
            scratch_shapes=[pltpu.VMEM((gb * rows, width), I32), pltpu.VMEM((gb * rows, width), F32),
                            pltpu.VMEM((gb * A_HEADS * rows, width), F32)]),
        out_shape=jax.ShapeDtypeStruct((bd, rows, W_MIX), F32),
        compiler_params=_cparams(("arbitrary",)),
        name="attn_sample",
    )(page_table.reshape(-1), qm, qst, wst, kp, vp, kinew,
      *([cache_kt] * n_refs), *([cache_vt] * n_refs), *([cache_ikt] * n_refs))
    return out[:, :ts]


def _rwkv_prep_math(p, prev, mu, w6, vecs, r_ref, lw_ref, k_ref, v_ref, a_ref, b_ref, g_ref):
    z = p + (prev - p) * mu
    r = z[:, 0:W_MIX]
    k = z[:, W_MIX:2 * W_MIX]
    v = z[:, 2 * W_MIX:3 * W_MIX]
    x6 = z[:, 3 * W_MIX:B_PROJ]
    lane = _iota(x6.shape, 1)
    act = jnp.where(lane < DECAY_LORA, jnp.tanh(x6),
                    jnp.where(lane < DECAY_LORA + AAA_LORA, x6, _sigmoid(x6)))
    lo = _mm(act, w6)
    w0, a0, kkw, ka = vecs[0:1], vecs[1:2], vecs[2:3], vecs[3:4]
    w_log = -_softplus(-(w0 + lo[:, 0:W_MIX])) - 0.5
    asig = _sigmoid(a0 + lo[:, W_MIX:2 * W_MIX])
    kk = k * kkw
    ss = _mm_xl(kk * kk, _head_block_ones(W_MIX, B_HEAD))
    kk = kk / jnp.maximum(jnp.sqrt(ss), 1e-12)
    r_ref[...] = r
    lw_ref[...] = -jnp.exp(w_log)
    k_ref[...] = k * (1.0 + (asig - 1.0) * ka)
    v_ref[...] = v
    a_ref[...] = -kk
    b_ref[...] = kk * asig
    g_ref[...] = lo[:, 2 * W_MIX:3 * W_MIX]


def _rwkv_prep_prompt_kernel(tiles_per_seq, p_ref, pprev_ref, mu_ref, w6_ref, vec_ref, *rest):
    outs, scr = rest[:7], rest[7]
    tm = p_ref.shape[0]
    p = p_ref[...]
    first = jnp.where(pl.program_id(0) % tiles_per_seq == 0, 0.0, pprev_ref[SUBLANES - 1:SUBLANES, :])
    scr[SUBLANES - 1:SUBLANES, :] = first
    scr[SUBLANES:SUBLANES + tm, :] = p
    prev = scr[SUBLANES - 1:SUBLANES - 1 + tm, :]
    _rwkv_prep_math(p, prev, mu_ref[...], w6_ref[...], vec_ref[...], *outs)


def _rwkv_prep_sample_kernel(bd, p_ref, shift_ref, mu_ref, w6_ref, vec_ref, *outs):
    p = p_ref[...]
    prev = jnp.concatenate([shift_ref[...], p[0:p.shape[0] - bd]], axis=0)
    _rwkv_prep_math(p, prev, mu_ref[...], w6_ref[...], vec_ref[...], *outs)


def _rwkv_prep_prompt(pb, lw, t, tm):
    n = pb.shape[0]
    tps = t // tm
    full = lambda a: pl.BlockSpec(a.shape, lambda i: (0, 0))
    row = lambda w: pl.BlockSpec((tm, w), lambda i: (i, 0))
    prev_spec = pl.BlockSpec((SUBLANES, B_PROJ), lambda i: (jnp.maximum(i * (tm // SUBLANES) - 1, 0), 0))
    return pl.pallas_call(
        functools.partial(_rwkv_prep_prompt_kernel, tps),
        grid=(n // tm,),
        in_specs=[row(B_PROJ), prev_spec, full(lw['rwkv_mu']), full(lw['w6']), full(lw['rwkv_vec'])],
        out_specs=[row(W_MIX)] * 7,
        out_shape=[jax.ShapeDtypeStruct((n, W_MIX), F32)] * 7,
        scratch_shapes=[pltpu.VMEM((tm + SUBLANES, B_PROJ), F32)],
        compiler_params=_cparams(("parallel",)),
        name="rwkv_prep_prompt",
    )(pb, pb, lw['rwkv_mu'], lw['w6'], lw['rwkv_vec'])


def _rwkv_prep_sample(pb, shift, lw, bd):
    n = pb.shape[0]
    full = lambda a: pl.BlockSpec(a.shape, lambda i: (0, 0))
    return pl.pallas_call(
        functools.partial(_rwkv_prep_sample_kernel, bd),
        grid=(1,),
        in_specs=[full(pb), full(shift), full(lw['rwkv_mu']), full(lw['w6']), full(lw['rwkv_vec'])],
        out_specs=[pl.BlockSpec((n, W_MIX), lambda i: (0, 0))] * 7,
        out_shape=[jax.ShapeDtypeStruct((n, W_MIX), F32)] * 7,
        compiler_params=_cparams(("arbitrary",)),
        name="rwkv_prep_sample",
    )(pb, shift, lw['rwkv_mu'], lw['w6'], lw['rwkv_vec'])


def _rwkv_post(y, r, k, v, g, vecs, jhead):
    lnx_g, lnx_b, rk = vecs[4:5], vecs[5:6], vecs[6:7]
    mu = _mm_xl(y, jhead) * (1.0 / B_HEAD)
    d = y - mu
    var = _mm_xl(d * d, jhead) * (1.0 / B_HEAD)
    yn = d * lax.rsqrt(var + LNX_EPS) * lnx_g + lnx_b
    bonus = _mm_xl(r * k * rk, jhead) * v
    return (yn + bonus) * g


def _rwkv_chunk_kernel(gb, r_ref, lw_ref, k_ref, v_ref, a_ref, b_ref, g_ref, vec_ref, y_ref, sfin_ref, s_scr):
    c = r_ref.shape[1]
    w = W_MIX
    ci = pl.program_id(1)

    @pl.when(ci == 0)
    def _():
        s_scr[...] = jnp.zeros(s_scr.shape, F32)

    row_c = _iota((c, c), 0)
    col_c = _iota((c, c), 1)
    incl = row_c >= col_c
    strict = row_c > col_c
    ltri = jnp.where(incl, 1.0, 0.0).astype(BF16)
    eye_c = jnp.where(row_c == col_c, 1.0, 0.0).astype(F32)
    jhead = _head_block_ones(w, B_HEAD)
    bm = jhead > 0.5
    eye_w = _iota((w, w), 0) == _iota((w, w), 1)
    lane = _iota((1, w), 1)
    vecs = vec_ref[...]
    levels = int(math.log2(c))

    masks = [lane // B_HEAD == h for h in range(B_HEADS)]
    pairs = [(bi, h) for bi in range(gb) for h in range(B_HEADS)]
    seq = []
    for bi in range(gb):
        r, lw, k, v, a, b = r_ref[bi], lw_ref[bi], k_ref[bi], v_ref[bi], a_ref[bi], b_ref[bi]
        cum = _mm_xr(ltri, lw)
        last = cum[c - 1:c, :]
        p_inv = jnp.exp(-cum)
        p_end = jnp.exp(last - cum)
        seq.append(dict(r=r, k=k, v=v, last=last, at=a * jnp.exp(cum - lw), rt=r * jnp.exp(cum),
                        bt=(b * p_inv).astype(BF16), kt=(k * p_inv).astype(BF16),
                        bpt=(b * p_end).T.astype(BF16), kpt=(k * p_end).T.astype(BF16), vb=v.astype(BF16)))
    n_mat, aak, brb, brk = {}, {}, {}, {}
    for bi, h in pairs:
        s = seq[bi]
        ar = jnp.concatenate([jnp.where(masks[h], s['at'], 0.0), jnp.where(masks[h], s['rt'], 0.0)],
                             axis=0).astype(BF16)
        mb = _dg(ar, s['bt'], NT)
        mk = _dg(ar, s['kt'], NT)
        n_mat[bi, h] = jnp.where(strict, mb[0:c], 0.0)
        aak[bi, h] = jnp.where(strict, mk[0:c], 0.0).astype(BF16)
        brb[bi, h] = jnp.where(incl, mb[c:2 * c], 0.0).astype(BF16)
        brk[bi, h] = jnp.where(incl, mk[c:2 * c], 0.0).astype(BF16)
    tinv = {p: eye_c + n_mat[p] for p in pairs}
    npow = n_mat
    for _ in range(levels - 1):
        npow = {p: _mm(npow[p], npow[p]) for p in pairs}
        tinv = {p: tinv[p] + _mm(tinv[p], npow[p]) for p in pairs}
    tinv = {p: tinv[p].astype(BF16) for p in pairs}
    x = {(bi, h): _dg(aak[bi, h], seq[bi]['vb'], NN).astype(BF16) for bi, h in pairs}
    for bi in range(gb):
        s = seq[bi]
        atb = s['at'].astype(BF16)
        wt = jnp.zeros((c, w), F32)
        u0 = jnp.zeros((c, w), F32)
        for h in range(B_HEADS):
            wt = wt + jnp.where(masks[h], _dg(tinv[bi, h], atb, NN), 0.0)
            u0 = u0 + jnp.where(masks[h], _dg(tinv[bi, h], x[bi, h], NN), 0.0)
        s['wtb'] = wt.astype(BF16)
        s['u0b'] = u0.astype(BF16)
    for bi in range(gb):
        s = seq[bi]
        ry = s['rt']
        y0 = jnp.zeros((c, w), F32)
        for h in range(B_HEADS):
            ry = ry + jnp.where(masks[h], _dg(brb[bi, h], s['wtb'], NN), 0.0)
            y0 = y0 + jnp.where(masks[h], _dg(brb[bi, h], s['u0b'], NN) + _dg(brk[bi, h], s['vb'], NN), 0.0)
        gt = jnp.where(eye_w, jnp.exp(s['last']), 0.0) + jnp.where(bm, _dg(s['bpt'], s['wtb'], NN), 0.0)
        ht = jnp.where(bm, _dg(s['bpt'], s['u0b'], NN) + _dg(s['kpt'], s['vb'], NN), 0.0)
        s0 = s_scr[bi]
        y = _mm3(ry, s0) + y0
        s_scr[bi] = _mm3(gt, s0) + ht
        y_ref[bi] = _rwkv_post(y, s['r'], s['k'], s['v'], g_ref[bi], vecs, jhead)

    @pl.when(ci == pl.num_programs(1) - 1)
    def _():
        fold = jnp.where(_iota((w, B_HEAD), 0) % B_HEAD == _iota((w, B_HEAD), 1), 1.0, 0.0)
        for bi in range(gb):
            sfin_ref[bi] = _mm_xl(s_scr[bi], fold)


def _rwkv_scan_prompt(arrs, lw, b, t, gb):
    c = RWKV_CHUNK
    a3 = [x.reshape(b, t, W_MIX) for x in arrs]
    blk = pl.BlockSpec((gb, c, W_MIX), lambda i, j: (i, j, 0))
    y, sfin = pl.pallas_call(
        functools.partial(_rwkv_chunk_kernel, gb),
        grid=(b // gb, t // c),
        in_specs=[blk] * 7 + [pl.BlockSpec(lw['rwkv_vec'].shape, lambda i, j: (0, 0))],
        out_specs=[blk, pl.BlockSpec((gb, W_MIX, B_HEAD), lambda i, j: (i, 0, 0))],
        out_shape=[jax.ShapeDtypeStruct((b, t, W_MIX), F32), jax.ShapeDtypeStruct((b, W_MIX, B_HEAD), F32)],
        scratch_shapes=[pltpu.VMEM((gb, W_MIX, W_MIX), F32)],
        compiler_params=_cparams(("parallel", "arbitrary")),
        name="rwkv_chunk",
    )(*a3, lw['rwkv_vec'])
    state = sfin.reshape(b, B_HEADS, B_HEAD, B_HEAD).transpose(0, 1, 3, 2)
    return y.reshape(b * t, W_MIX), state


def _rwkv_step_kernel(gb, r_ref, lw_ref, k_ref, v_ref, a_ref, b_ref, g_ref, s_ref, vec_ref, y_ref, so_ref):
    ts = r_ref.shape[0]
    w = W_MIX
    jhead = _head_block_ones(w, B_HEAD)
    q1 = jnp.where(_iota((B_HEAD, w), 0) == _iota((B_HEAD, w), 1) % B_HEAD, 1.0, 0.0)
    q1g = jnp.concatenate([q1] * gb, axis=0)
    vecs = vec_ref[...]

    def rows(x):
        return jnp.concatenate([jnp.broadcast_to(x[bi:bi + 1, :], (B_HEAD, w)) for bi in range(gb)], axis=0)

    s = s_ref[...].reshape(gb * B_HEAD, w)
    for t in range(ts):
        r_t, k_t, v_t = r_ref[t], k_ref[t], v_ref[t]
        sa = _mm_xl(s * rows(a_ref[t]), jhead)
        vcol = _mm_xl(q1g * rows(v_t), jhead)
        s = s * rows(jnp.exp(lw_ref[t])) + sa * rows(b_ref[t]) + vcol * rows(k_t)
        yb = _mm_xl(s * rows(r_t), jhead)
        y_t = jnp.sum((yb * q1g).reshape(gb, B_HEAD, w), axis=1)
        y_ref[t] = _rwkv_post(y_t, r_t, k_t, v_t, g_ref[t], vecs, jhead)
    so_ref[...] = s.reshape(gb, B_HEAD, w)


def _rwkv_scan_sample(arrs, state, lw, bd, ts, gb):
    a3 = [x.reshape(ts, bd, W_MIX) for x in arrs]
    s_in = state.transpose(0, 2, 1, 3).reshape(bd, B_HEAD, W_MIX)
    blk = pl.BlockSpec((ts, gb, W_MIX), lambda i: (0, i, 0))
    sblk = pl.BlockSpec((gb, B_HEAD, W_MIX), lambda i: (i, 0, 0))
    y, s_out = pl.pallas_call(
        functools.partial(_rwkv_step_kernel, gb),
        grid=(bd // gb,),
        in_specs=[blk] * 7 + [sblk, pl.BlockSpec(lw['rwkv_vec'].shape, lambda i: (0, 0))],
        out_specs=[blk, sblk],
        out_shape=[jax.ShapeDtypeStruct((ts, bd, W_MIX), F32), jax.ShapeDtypeStruct((bd, B_HEAD, W_MIX), F32)],
        compiler_params=_cparams(("parallel",)),
        name="rwkv_step",
    )(*a3, s_in, lw['rwkv_vec'])
    new_state = s_out.reshape(bd, B_HEAD, B_HEADS, B_HEAD).transpose(0, 2, 1, 3)
    return y.reshape(ts * bd, W_MIX), new_state


def _lru_inputs(xc, gate_w_ref, vec):
    gates = _mm(xc, gate_w_ref[...])
    gate_x = _sigmoid(gates[:, 0:W_MIX] + vec[0:1])
    gate_a = _sigmoid(gates[:, W_MIX:2 * W_MIX] + vec[1:2])
    log_a = -LRU_C * gate_a * _softplus(-vec[2:3])
    a2 = jnp.exp(2.0 * log_a)
    neg_expm1 = -jnp.tanh(log_a) * (a2 + 1.0)
    u = jnp.sqrt(neg_expm1) * gate_x * xc
    return jnp.exp(log_a), u


def _cd_prompt_kernel(pc_ref, pd_ref, cw_ref, dw_ref, gw_ref, vec_ref,
                      yc_ref, yd_ref, bufc_ref, bufd_ref, h_ref, gx_scr, xb_scr, a_scr, u_scr, hs_scr, h_scr):
    tt = pc_ref.shape[0]
    ti = pl.program_id(1)
    pad = SUBLANES

    @pl.when(ti == 0)
    def _():
        gx_scr[0:pad, :] = jnp.zeros((pad, W_MIX), F32)
        xb_scr[0:pad, :] = jnp.zeros((pad, W_MIX), F32)
        h_scr[...] = jnp.zeros(h_scr.shape, F32)

    @pl.when(ti > 0)
    def _():
        gx_scr[0:pad, :] = gx_scr[tt:tt + pad, :]
        xb_scr[0:pad, :] = xb_scr[tt:tt + pad, :]

    vec = vec_ref[...]
    gx_scr[pad:pad + tt, :] = pc_ref[:, W_MIX:2 * W_MIX] * pc_ref[:, 2 * W_MIX:3 * W_MIX]
    conv = cw_ref[0:1, :] * gx_scr[pad - 2:pad - 2 + tt, :]
    for j in range(1, C_CONV):
        conv = conv + cw_ref[j:j + 1, :] * gx_scr[pad - 2 + j:pad - 2 + j + tt, :]
    yc_ref[...] = pc_ref[:, 0:W_MIX] * conv
    bufc_ref[...] = gx_scr[pad + tt - (C_CONV - 1):pad + tt, :]
    xb_scr[pad:pad + tt, :] = pd_ref[:, 0:W_MIX]
    xc = dw_ref[0:1, :] * xb_scr[pad - 3:pad - 3 + tt, :]
    for j in range(1, D_CONV):
        xc = xc + dw_ref[j:j + 1, :] * xb_scr[pad - 3 + j:pad - 3 + j + tt, :]
    xc = xc + vec[3:4]
    bufd_ref[...] = xb_scr[pad + tt - (D_CONV - 1):pad + tt, :]
    a, u = _lru_inputs(xc, gw_ref, vec)
    a_scr[...] = a
    u_scr[...] = u
    row8 = _iota((SUBLANES, W_MIX), 0)

    def group(gi, h):
        base = pl.multiple_of(gi * SUBLANES, SUBLANES)
        a8 = a_scr[pl.ds(base, SUBLANES), :]
        u8 = u_scr[pl.ds(base, SUBLANES), :]
        hs8 = jnp.zeros((SUBLANES, W_MIX), F32)
        for j in range(SUBLANES):
            h = a8[j:j + 1, :] * h + u8[j:j + 1, :]
            hs8 = jnp.where(row8 == j, h, hs8)
        hs_scr[pl.ds(base, SUBLANES), :] = hs8
        return h

    h = lax.fori_loop(0, tt // SUBLANES, group, h_scr[...])
    h_scr[...] = h
    h_ref[...] = h
    yd_ref[...] = hs_scr[...] * _gelu_tanh(pd_ref[:, W_MIX:2 * W_MIX])


def _cd_prompt(pc, pd, lw, b, t, tt):
    nt = t // tt
    row = lambda w: pl.BlockSpec((tt, w), lambda bi, ti: (bi * nt + ti, 0))
    full = lambda a: pl.BlockSpec(a.shape, lambda bi, ti: (0, 0))
    per_b = lambda r: pl.BlockSpec((None, r, W_MIX), lambda bi, ti: (bi, 0, 0))
    big = lambda: pltpu.VMEM((tt + 2 * SUBLANES, W_MIX), F32)
    tile = lambda: pltpu.VMEM((tt, W_MIX), F32)
    return pl.pallas_call(
        _cd_prompt_kernel,
        grid=(b, nt),
        in_specs=[row(3 * W_MIX), row(2 * W_MIX), full(lw['convc_w']), full(lw['convd_w']), full(lw['lru_gw']),
                  full(lw['lru_vec'])],
        out_specs=[row(W_MIX), row(W_MIX), per_b(C_CONV - 1), per_b(D_CONV - 1), per_b(1)],
        out_shape=[jax.ShapeDtypeStruct((b * t, W_MIX), F32), jax.ShapeDtypeStruct((b * t, W_MIX), F32),
                   jax.ShapeDtypeStruct((b, C_CONV - 1, W_MIX), F32),
                   jax.ShapeDtypeStruct((b, D_CONV - 1, W_MIX), F32), jax.ShapeDtypeStruct((b, 1, W_MIX), F32)],
        scratch_shapes=[big(), big(), tile(), tile(), tile(), pltpu.VMEM((1, W_MIX), F32)],
        compiler_params=_cparams(("parallel", "arbitrary")),
        name="cd_prompt",
    )(pc, pd, lw['convc_w'], lw['convd_w'], lw['lru_gw'], lw['lru_vec'])


def _cd_sample_kernel(pc_ref, pd_ref, bc_ref, bd_ref, h0_ref, cw_ref, dw_ref, gw_ref, vec_ref,
                      yc_ref, yd_ref, bufc_ref, bufd_ref, h_ref):
    ts = pc_ref.shape[0]
    vec = vec_ref[...]
    gx = [bc_ref[j] for j in range(C_CONV - 1)]
    gx += [pc_ref[t][:, W_MIX:2 * W_MIX] * pc_ref[t][:, 2 * W_MIX:3 * W_MIX] for t in range(ts)]
    for t in range(ts):
        conv = cw_ref[0:1, :] * gx[t]
        for j in range(1, C_CONV):
            conv = conv + cw_ref[j:j + 1, :] * gx[t + j]
        yc_ref[t] = pc_ref[t][:, 0:W_MIX] * conv
    for j in range(C_CONV - 1):
        bufc_ref[j] = gx[ts + j]
    xb = [bd_ref[j] for j in range(D_CONV - 1)] + [pd_ref[t][:, 0:W_MIX] for t in range(ts)]
    h = h0_ref[...]
    for t in range(ts):
        xc = dw_ref[0:1, :] * xb[t]
        for j in range(1, D_CONV):
            xc = xc + dw_ref[j:j + 1, :] * xb[t + j]
        xc = xc + vec[3:4]
        a, u = _lru_inputs(xc, gw_ref, vec)
        h = a * h + u
        yd_ref[t] = h * _gelu_tanh(pd_ref[t][:, W_MIX:2 * W_MIX])
    for j in range(D_CONV - 1):
        bufd_ref[j] = xb[ts + j]
    h_ref[...] = h


def _cd_sample(pc, pd, buf_c, buf_d, h0, lw, bd, ts):
    args = (pc, pd, buf_c, buf_d, h0, lw['convc_w'], lw['convd_w'], lw['lru_gw'], lw['lru_vec'])
    full = lambda a: pl.BlockSpec(a.shape, lambda i: (0,) * a.ndim)
    shapes = [(ts, bd, W_MIX), (ts, bd, W_MIX), (C_CONV - 1, bd, W_MIX), (D_CONV - 1, bd, W_MIX), (bd, W_MIX)]
    return pl.pallas_call(
        _cd_sample_kernel,
        grid=(1,),
        in_specs=[full(a) for a in args],
        out_specs=[pl.BlockSpec(s, lambda i, n=len(s): (0,) * n) for s in shapes],
        out_shape=[jax.ShapeDtypeStruct(s, F32) for s in shapes],
        compiler_params=_cparams(("arbitrary",)),
        name="cd_sample",
    )(*args)


def _outproj_kernel(x_ref, ya_ref, yb_ref, yc_ref, yd_ref, w_ref, ln_ref, o_ref):
    mix = _dg(ya_ref[...].astype(BF16), w_ref[0:W_MIX, :], NN)
    mix = mix + _dg(yb_ref[...].astype(BF16), w_ref[W_MIX:2 * W_MIX, :], NN)
    mix = mix + _dg(yc_ref[...].astype(BF16), w_ref[2 * W_MIX:3 * W_MIX, :], NN)
    mix = mix + _dg(yd_ref[...].astype(BF16), w_ref[3 * W_MIX:4 * W_MIX, :], NN)
    o_ref[...] = _layer_norm_rows(DEEPNORM_ALPHA * x_ref[...] + mix, ln_ref[0:1, :], ln_ref[1:2, :])


def _outproj(x2d, ys, lw, tm):
    n, dm = x2d.shape
    row = lambda w: pl.BlockSpec((tm, w), lambda i: (i, 0))
    full = lambda a: pl.BlockSpec(a.shape, lambda i: (0, 0))
    return pl.pallas_call(
        _outproj_kernel,
        grid=(n // tm,),
        in_specs=[row(dm)] + [row(W_MIX)] * 4 + [full(lw['w_out']), full(lw['ln1'])],
        out_specs=row(dm),
        out_shape=jax.ShapeDtypeStruct((n, dm), F32),
        compiler_params=_cparams(("parallel",)),
        name="outproj",
    )(x2d, *ys, lw['w_out'], lw['ln1'])


def _route(x, wr_h, wr_l, bias_col):
    xh, xl = _split2(x)
    logits = _dg(xh, wr_h, NN) + (_dg(xh, wr_l, NN) + _dg(xl, wr_h, NN))
    st = _sigmoid(logits.T[0:N_EXPERTS, :])
    sel = st + bias_col
    rows = [sel[e:e + 1, :] for e in range(N_EXPERTS)]
    in_top2 = []
    for e in range(N_EXPERTS):
        g0 = (e // EXPERTS_PER_GROUP) * EXPERTS_PER_GROUP
        rank = jnp.zeros(rows[e].shape, F32)
        for o in range(g0, g0 + EXPERTS_PER_GROUP):
            if o == e:
                continue
            ahead = (rows[o] >= rows[e]) if o < e else (rows[o] > rows[e])
            rank = rank + jnp.where(ahead, 1.0, 0.0)
        in_top2.append(rank < float(2))
    gscore = []
    for g in range(N_GROUPS):
        acc = jnp.zeros(rows[0].shape, F32)
        for e in range(g * EXPERTS_PER_GROUP, (g + 1) * EXPERTS_PER_GROUP):
            acc = acc + jnp.where(in_top2[e], rows[e], 0.0)
        gscore.append(acc)
    best = gscore[0]
    gidx = jnp.zeros(best.shape, I32)
    for g in range(1, N_GROUPS):
        better = gscore[g] > best
        best = jnp.where(better, gscore[g], best)
        gidx = jnp.where(better, g, gidx)
    picked = [jnp.where(in_top2[e] & (gidx == e // EXPERTS_PER_GROUP), st[e:e + 1, :], 0.0)
              for e in range(N_EXPERTS)]
    total = picked[0]
    for e in range(1, N_EXPERTS):
        total = total + picked[e]
    return jnp.concatenate(picked, axis=0) / total


def _moe_kernel(x_ref, wrh_ref, wrl_ref, rb_ref, wgu_ref, wd_ref, ln_ref, o_ref, xb_scr, gate_scr, acc_scr):
    e = pl.program_id(1)

    @pl.when(e == 0)
    def _():
        x = x_ref[...]
        xb_scr[...] = x.astype(BF16)
        gates = _route(x, wrh_ref[...], wrl_ref[...], rb_ref[...])
        pad = jnp.zeros((LANES - N_EXPERTS, gates.shape[1]), F32)
        gate_scr[...] = jnp.concatenate([gates, pad], axis=0).T
        acc_scr[...] = jnp.zeros(acc_scr.shape, F32)

    gu = _dg(xb_scr[...], wgu_ref[...], NN)
    hidden = gu[:, 0:D_EXPERT]
    hidden = hidden * _sigmoid(hidden) * gu[:, D_EXPERT:2 * D_EXPERT]
    lane = _iota(gate_scr.shape, 1)
    ge = jnp.sum(jnp.where(lane == e, gate_scr[...], 0.0), axis=1, keepdims=True)
    acc_scr[...] += _dg((hidden * ge).astype(BF16), wd_ref[...], NN)

    @pl.when(e == pl.num_programs(1) - 1)
    def _():
        o_ref[...] = _layer_norm_rows(DEEPNORM_ALPHA * x_ref[...] + acc_scr[...], ln_ref[0:1, :], ln_ref[1:2, :])


def _moe(x2d, lw, shared, tm):
    n, dm = x2d.shape
    row = pl.BlockSpec((tm, dm), lambda i, e: (i, 0))
    full = lambda a: pl.BlockSpec(a.shape, lambda i, e: (0, 0))
    return pl.pallas_call(
        _moe_kernel,
        grid=(n // tm, N_EXPERTS),
        in_specs=[row, full(shared['wr_h']), full(shared['wr_l']), full(shared['router_bias']),
                  pl.BlockSpec((None, dm, 2 * D_EXPERT), lambda i, e: (e, 0, 0)),
                  pl.BlockSpec((None, D_EXPERT, dm), lambda i, e: (e, 0, 0)), full(lw['ln2'])],
        out_specs=row,
        out_shape=jax.ShapeDtypeStruct((n, dm), F32),
        scratch_shapes=[pltpu.VMEM((tm, dm), BF16), pltpu.VMEM((tm, LANES), F32), pltpu.VMEM((tm, dm), F32)],
        compiler_params=_cparams(("parallel", "arbitrary")),
        name="moe",
    )(x2d, shared['wr_h'], shared['wr_l'], shared['router_bias'], lw['moe_wgu'], lw['moe_wd'], lw['ln2'])


def _rope_tables(pos, head_dim, rot_dim):
    half = rot_dim // 2
    inv_freq = ROPE_THETA ** (-jnp.arange(half, dtype=F32) / half)
    ang = pos.astype(F32)[:, None] * inv_freq[None, :]
    cos, sin = jnp.cos(ang), jnp.sin(ang)
    lane = jnp.arange(LANES)
    within = lane % head_dim
    idx = within % half
    first = within < half
    second = (within >= half) & (within < rot_dim)
    c = jnp.where((first | second)[None, :], cos[:, idx], 1.0)
    sm = jnp.where(first[None, :], -sin[:, idx], 0.0)
    sp = jnp.where(second[None, :], sin[:, idx], 0.0)
    return jnp.concatenate([c, sm, sp], axis=1)


def _block_diag_heads(w):
    h, d, _ = w.shape
    eye = jnp.eye(h, dtype=w.dtype)
    return (eye[:, None, :, None] * w[:, :, None, :]).reshape(h * d, h * d)


def _row_pad(v, width=LANES):
    return jnp.pad(v, (0, width - v.shape[0]))[None, :]


def _prep_layer(l, p):
    w_in = p['w_in'][l]
    ia = 3 * W_MIX
    wi_cols = jnp.pad(w_in[:, ia:A_COLS], ((0, 0), (0, 2 * LANES - (A_COLS - ia))))
    wih = wi_cols.astype(BF16)
    wil = (wi_cols - wih.astype(F32)).astype(BF16)
    b0 = A_COLS
    c0 = b0 + B_PROJ
    d0 = c0 + 3 * W_MIX
    w6 = jnp.zeros((LANES, 3 * W_MIX), F32)
    w6 = w6.at[0:DECAY_LORA, 0:W_MIX].set(p['rwkv_w2'][l])
    w6 = w6.at[DECAY_LORA:DECAY_LORA + AAA_LORA, W_MIX:2 * W_MIX].set(p['rwkv_a2'][l])
    w6 = w6.at[DECAY_LORA + AAA_LORA:LANES, 2 * W_MIX:3 * W_MIX].set(p['rwkv_g2'][l])
    zeros = jnp.zeros((W_MIX,), F32)
    return {
        'wa': w_in[:, 0:ia].astype(BF16), 'wih': wih, 'wil': wil,
        'wb': w_in[:, b0:c0].astype(BF16), 'wc': w_in[:, c0:d0].astype(BF16), 'wd': w_in[:, d0:].astype(BF16),
        'idx_ln': jnp.concatenate([_row_pad(p['idx_ln_g'][l]), _row_pad(p['idx_ln_b'][l])], axis=0),
        'rwkv_mu': p['rwkv_mu'][l][None, :],
        'w6': w6.astype(BF16),
        'rwkv_vec': jnp.stack([p['rwkv_w0'][l], p['rwkv_a0'][l], p['rwkv_kk'][l], p['rwkv_ka'][l],
                               p['rwkv_lnx_g'][l], p['rwkv_lnx_b'][l], p['rwkv_rk'][l].reshape(-1), zeros]),
        'convc_w': p['convc_w'][l], 'convd_w': p['convd_w'][l],
        'lru_gw': jnp.concatenate([_block_diag_heads(p['lru_gx_w'][l]), _block_diag_heads(p['lru_ga_w'][l])],
                                  axis=1).astype(BF16),
        'lru_vec': jnp.stack([p['lru_gx_b'][l], p['lru_ga_b'][l], p['lru_lam'][l], p['convd_b'][l]]),
        'w_out': p['w_out'][l].astype(BF16),
        'ln1': jnp.stack([p['ln1_g'][l], p['ln1_b'][l]]),
        'ln2': jnp.stack([p['ln2_g'][l], p['ln2_b'][l]]),
        'moe_wgu': jnp.concatenate([p['moe_w_gate'][l], p['moe_w_up'][l]], axis=-1).astype(BF16),
        'moe_wd': p['moe_w_down'][l].astype(BF16),
    }


def _prompt_layer(x2d, lw, shared, tabs, b, t):
    _, k, v, _, kiw, pb, pc, pd, qat, kbf, vbf, qi3, ki3 = _inproj(x2d, lw, tabs, t, min(512, t))
    ya = _attn_prompt(qat, qi3, kiw, kbf, vbf, ki3, b, t)
    arrs = _rwkv_prep_prompt(pb, lw, t, 256)
    yb, rwkv_state = _rwkv_scan_prompt(arrs, lw, b, t, min(4, b))
    yc, yd, bufc, bufd, h = _cd_prompt(pc, pd, lw, b, t, min(512, t))
    x1 = _outproj(x2d, (ya, yb, yc, yd), lw, 512)
    x2 = _moe(x1, lw, shared, min(1024, b * t))
    st = (k.reshape(b, t, A_HEADS, A_HEAD_DIM), v.reshape(b, t, A_HEADS, A_HEAD_DIM),
          kiw[:, 0:IDX_DIM].reshape(b, t, IDX_DIM), rwkv_state,
          pb.reshape(b, t, B_PROJ)[:, t - 1:t], bufc, bufd, h.reshape(b, W_MIX))
    return x2, st


def _sample_layer(l, x2d, lw, shared, tabs, bd, ts, caches, page_table, states):
    n = bd * ts
    rwkv_state, shift, conv_c, conv_d, lru_h = states
    q, k, v, qi, kiw, pb, pc, pd = _inproj(x2d, lw, tabs, n, 256)[0:8]
    bm = lambda a: a.reshape(ts, bd, a.shape[-1]).transpose(1, 0, 2)
    ya = _attn_sample(l, bm(q), bm(qi), bm(kiw), bm(k), bm(v), page_table, *caches)
    ya = ya.transpose(1, 0, 2).reshape(n, W_MIX)
    arrs = _rwkv_prep_sample(pb, shift.reshape(bd, B_PROJ), lw, bd)
    yb, new_rwkv = _rwkv_scan_sample(arrs, rwkv_state, lw, bd, ts, 16)
    yc, yd, bufc, bufd, h = _cd_sample(pc.reshape(ts, bd, 3 * W_MIX), pd.reshape(ts, bd, 2 * W_MIX),
                                       conv_c.transpose(1, 0, 2), conv_d.transpose(1, 0, 2), lru_h, lw, bd, ts)
    x1 = _outproj(x2d, (ya, yb, yc.reshape(n, W_MIX), yd.reshape(n, W_MIX)), lw, 256)
    x2 = _moe(x1, lw, shared, n)
    kb, vb, kib = bm(k), bm(v), bm(kiw)
    st = (kb.reshape(bd, ts, A_HEADS, A_HEAD_DIM), vb.reshape(bd, ts, A_HEADS, A_HEAD_DIM), kib[:, :, 0:IDX_DIM],
          new_rwkv, pb.reshape(ts, bd, B_PROJ)[ts - 1][:, None, :], bufc.transpose(1, 0, 2),
          bufd.transpose(1, 0, 2), h)
    return x2, st


def kernel(x_prompt, x_sample, cache_k, cache_v, cache_ik, page_table, state_rwkv, state_rwkv_shift, state_conv_c, state_conv_d, state_lru, w_in, w_out, idx_ln_g, idx_ln_b, rwkv_mu, rwkv_w0, rwkv_w2, rwkv_a0, rwkv_a2, rwkv_g2, rwkv_kk, rwkv_ka, rwkv_rk, rwkv_lnx_g, rwkv_lnx_b, convc_w, convd_w, convd_b, lru_gx_w, lru_gx_b, lru_ga_w, lru_ga_b, lru_lam, ln1_g, ln1_b, ln2_g, ln2_b, w_router, router_bias, moe_w_gate, moe_w_up, moe_w_down):
    p = dict(w_in=w_in, w_out=w_out, idx_ln_g=idx_ln_g, idx_ln_b=idx_ln_b, rwkv_mu=rwkv_mu, rwkv_w0=rwkv_w0,
             rwkv_w2=rwkv_w2, rwkv_a0=rwkv_a0, rwkv_a2=rwkv_a2, rwkv_g2=rwkv_g2, rwkv_kk=rwkv_kk, rwkv_ka=rwkv_ka,
             rwkv_rk=rwkv_rk, rwkv_lnx_g=rwkv_lnx_g, rwkv_lnx_b=rwkv_lnx_b, convc_w=convc_w, convd_w=convd_w,
             convd_b=convd_b, lru_gx_w=lru_gx_w, lru_gx_b=lru_gx_b, lru_ga_w=lru_ga_w, lru_ga_b=lru_ga_b,
             lru_lam=lru_lam, ln1_g=ln1_g, ln1_b=ln1_b, ln2_g=ln2_g, ln2_b=ln2_b, moe_w_gate=moe_w_gate,
             moe_w_up=moe_w_up, moe_w_down=moe_w_down)
    depth = w_in.shape[0]
    b, t, dm = x_prompt.shape
    bd, ts, _ = x_sample.shape
    past_len = page_table.shape[1] * PAGE_SIZE

    wr = jnp.pad(w_router, ((0, 0), (0, LANES - N_EXPERTS)))
    wr_h = wr.astype(BF16)
    shared = {'wr_h': wr_h, 'wr_l': (wr - wr_h.astype(F32)).astype(BF16), 'router_bias': router_bias[:, None]}

    pos_p = jnp.arange(t)
    pos_s = jnp.repeat(past_len + jnp.arange(ts), bd)
    tabs_p = (_rope_tables(pos_p, A_HEAD_DIM, ROT_DIM), _rope_tables(pos_p, IDX_DIM, IDX_ROT_DIM),
              _rope_tables(pos_p, LANES, IDX_ROT_DIM))
    tabs_s = (_rope_tables(pos_s, A_HEAD_DIM, ROT_DIM), _rope_tables(pos_s, IDX_DIM, IDX_ROT_DIM),
              _rope_tables(pos_s, LANES, IDX_ROT_DIM))

    n_phys = cache_k.shape[1]
    caches_t = (cache_k.transpose(0, 1, 3, 4, 2).reshape(depth, n_phys, W_MIX, PAGE_SIZE),
                cache_v.transpose(0, 1, 3, 4, 2).reshape(depth, n_phys, W_MIX, PAGE_SIZE),
                cache_ik.transpose(0, 1, 3, 2))
    xp = x_prompt.reshape(b * t, dm)
    xs = x_sample.transpose(1, 0, 2).reshape(ts * bd, dm)
    new_p = [[] for _ in range(8)]
    new_s = [[] for _ in range(8)]
    for l in range(depth):
        lw = _prep_layer(l, p)
        xp, st_p = _prompt_layer(xp, lw, shared, tabs_p, b, t)
        xs, st_s = _sample_layer(l, xs, lw, shared, tabs_s, bd, ts, caches_t, page_table,
                                 (state_rwkv[l], state_rwkv_shift[l], state_conv_c[l], state_conv_d[l],
                                  state_lru[l]))
        for i in range(8):
            new_p[i].append(st_p[i])
            new_s[i].append(st_s[i])
    y_p = xp.reshape(b, t, dm)
    y_s = xs.reshape(ts, bd, dm).transpose(1, 0, 2)
    return (y_p, y_s) + tuple(jnp.stack(a) for a in new_p) + tuple(jnp.stack(a) for a in new_s)
```

```python
import functools
import math

import jax
import jax.numpy as jnp
from jax import lax
from jax.experimental import pallas as pl
from jax.experimental.pallas import tpu as pltpu

F32 = jnp.float32
BF16 = jnp.bfloat16
I32 = jnp.int32

A_HEADS = 4
A_HEAD_DIM = 64
ROT_DIM = 16
IDX_HEADS = 4
IDX_DIM = 32
IDX_ROT_DIM = 8
ROPE_THETA = 500000.0
TOPK_MAX = 256
PAGE_SIZE = 128
W_MIX = 256
B_HEAD = 64
B_HEADS = 4
DECAY_LORA = 32
AAA_LORA = 32
GATE_LORA = 64
B_PROJ = 3 * W_MIX + DECAY_LORA + AAA_LORA + GATE_LORA
LNX_EPS = 64e-5
C_CONV = 3
D_CONV = 4
D_HEADS = 4
LRU_C = 8.0
N_EXPERTS = 16
N_GROUPS = 4
EXPERTS_PER_GROUP = 4
D_EXPERT = 256
LN_EPS = 1e-5
DEPTH = 4
DEEPNORM_ALPHA = (2 * DEPTH) ** 0.25

A_COLS = 3 * W_MIX + IDX_HEADS * IDX_DIM + IDX_DIM + IDX_HEADS
A_PAD = 1024

LANES = 128
SUBLANES = 8
VMEM_LIMIT = 56 * 1024 * 1024

INT_MIN = -2147483648
NEG_BIG = -1e30
RWKV_CHUNK = 128


def _cparams(sem):
    return pltpu.CompilerParams(dimension_semantics=sem, vmem_limit_bytes=VMEM_LIMIT)


def _dg(a, b, dims):
    return lax.dot_general(a, b, (dims, ((), ())), preferred_element_type=F32)


NN = ((1,), (0,))
NT = ((1,), (1,))


def _mm(a, b, dims=NN):
    return _dg(a.astype(BF16), b.astype(BF16), dims)


def _split2(x):
    hi = x.astype(BF16)
    lo = (x - hi.astype(F32)).astype(BF16)
    return hi, lo


def _split3(x):
    hi = x.astype(BF16)
    r1 = x - hi.astype(F32)
    mid = r1.astype(BF16)
    lo = (r1 - mid.astype(F32)).astype(BF16)
    return hi, mid, lo


def _mm3(a, b, dims=NN):
    ah, al = _split2(a)
    bh, bl = _split2(b)
    return _dg(ah, bh, dims) + (_dg(ah, bl, dims) + _dg(al, bh, dims))


def _mm_xl(a, b_exact, dims=NN):
    h, m, l = _split3(a)
    bb = b_exact.astype(BF16)
    return _dg(h, bb, dims) + (_dg(m, bb, dims) + _dg(l, bb, dims))


def _mm_xr(a_exact, b, dims=NN):
    h, m, l = _split3(b)
    aa = a_exact.astype(BF16)
    return _dg(aa, h, dims) + (_dg(aa, m, dims) + _dg(aa, l, dims))


def _iota(shape, dim):
    return lax.broadcasted_iota(I32, shape, dim)


def _head_block_ones(n, blk):
    r = _iota((n, n), 0) // blk
    c = _iota((n, n), 1) // blk
    return jnp.where(r == c, 1.0, 0.0).astype(F32)


def _layer_norm_rows(x, g, b):
    mu = jnp.mean(x, axis=-1, keepdims=True)
    d = x - mu
    var = jnp.mean(d * d, axis=-1, keepdims=True)
    return d * lax.rsqrt(var + LN_EPS) * g + b


def _softplus(x):
    return jnp.maximum(x, 0.0) + jnp.log1p(jnp.exp(-jnp.abs(x)))


def _sigmoid(x):
    return 1.0 / (1.0 + jnp.exp(-x))


def _gelu_tanh(x):
    return 0.5 * x * (1.0 + jnp.tanh(math.sqrt(2.0 / math.pi) * (x + 0.044715 * (x * x * x))))


def _rope_block(blk, tab, half):
    c = tab[:, 0:LANES]
    sm = tab[:, LANES:2 * LANES]
    sp = tab[:, 2 * LANES:3 * LANES]
    return blk * c + pltpu.roll(blk, LANES - half, 1) * sm + pltpu.roll(blk, half, 1) * sp


def _inproj_kernel(x_ref, wa_ref, wih_ref, wil_ref, wb_ref, wc_ref, wd_ref, tqk_ref, tqi_ref, tki_ref, ln_ref,
                   pq_ref, pk_ref,
                   q_ref, k_ref, v_ref, qi_ref, kiw_ref, pb_ref, pc_ref, pd_ref,
                   qat_ref, kbf_ref, vbf_ref, qi3_ref, ki3_ref):
    x = x_ref[...]
    xh = x.astype(BF16)
    xl = (x - xh.astype(F32)).astype(BF16)
    acc = _dg(xh, wa_ref[...], NN)
    tqk = tqk_ref[...]
    for j in range(2):
        qj = _rope_block(acc[:, j * LANES:(j + 1) * LANES], tqk, ROT_DIM // 2)
        kj = _rope_block(acc[:, (2 + j) * LANES:(3 + j) * LANES], tqk, ROT_DIM // 2)
        q_ref[:, j * LANES:(j + 1) * LANES] = qj
        k_ref[:, j * LANES:(j + 1) * LANES] = kj
        qat_ref[:, j * LANES:(j + 1) * LANES] = (qj * (A_HEAD_DIM ** -0.5)).astype(BF16)
        kbf_ref[:, j * LANES:(j + 1) * LANES] = kj.astype(BF16)
    v = acc[:, 4 * LANES:6 * LANES]
    v_ref[...] = v
    vbf_ref[...] = v.astype(BF16)
    wih = wih_ref[...]
    acci = _dg(xh, wih, NN) + (_dg(xh, wil_ref[...], NN) + _dg(xl, wih, NN))
    qi = _rope_block(acci[:, 0:LANES], tqi_ref[...], IDX_ROT_DIM // 2)
    qi_ref[...] = qi
    qih, qil = _split2(qi)
    qi3_ref[...] = _dg(jnp.concatenate([qih, qil], axis=1), pq_ref[...], NN).astype(BF16)
    kw = acci[:, LANES:2 * LANES]
    lane = _iota(kw.shape, 1)
    in_ki = lane < IDX_DIM
    mu = jnp.sum(jnp.where(in_ki, kw, 0.0), axis=-1, keepdims=True) * (1.0 / IDX_DIM)
    d = jnp.where(in_ki, kw - mu, 0.0)
    var = jnp.sum(d * d, axis=-1, keepdims=True) * (1.0 / IDX_DIM)
    kn = d * lax.rsqrt(var + LN_EPS) * ln_ref[0:1, :] + ln_ref[1:2, :]
    kr = _rope_block(kn, tki_ref[...], IDX_ROT_DIM // 2)
    kr = jnp.where(in_ki, kr, 0.0)
    kiw_ref[...] = jnp.where(in_ki, kr, jnp.where(lane < IDX_DIM + IDX_HEADS, kw * (IDX_HEADS ** -0.5), 0.0))
    krh, krl = _split2(kr)
    ki3_ref[...] = _dg(jnp.concatenate([krh, krl], axis=1), pk_ref[...], NN).astype(BF16)
    pb_ref[...] = _dg(xh, wb_ref[...], NN)
    pc_ref[...] = _dg(xh, wc_ref[...], NN)
    pd_ref[...] = _dg(xh, wd_ref[...], NN)


def _placement_matrices():
    s = jnp.arange(2 * LANES)
    is_lo = s >= LANES
    h, d = (s % LANES) // IDX_DIM, s % IDX_DIM
    tq = jnp.arange(IDX_HEADS * LANES)
    q_hi = (tq[None, :] == (LANES * h + d)[:, None]) | (tq[None, :] == (LANES * h + 2 * IDX_DIM + d)[:, None])
    q_lo = tq[None, :] == (LANES * h + IDX_DIM + d)[:, None]
    pq = jnp.where(is_lo[:, None], q_lo, q_hi)
    tk = jnp.arange(LANES)
    src_ok = ((s % LANES) < IDX_DIM)[:, None]
    k_hi = (tk[None, :] == d[:, None]) | (tk[None, :] == (IDX_DIM + d)[:, None])
    k_lo = tk[None, :] == (2 * IDX_DIM + d)[:, None]
    pk = jnp.where(is_lo[:, None], k_lo, k_hi) & src_ok
    return pq.astype(BF16), pk.astype(BF16)


def _inproj(x2d, lw, tabs, tab_rows, tm):
    n, dm = x2d.shape
    nt = tab_rows // tm
    pq, pk = _placement_matrices()
    full = lambda a: pl.BlockSpec(a.shape, lambda i: (0, 0))
    row = lambda w: pl.BlockSpec((tm, w), lambda i: (i, 0))
    tab = lambda: pl.BlockSpec((tm, 3 * LANES), lambda i: (i % nt, 0))
    outs = [(W_MIX, F32), (W_MIX, F32), (W_MIX, F32), (LANES, F32), (LANES, F32), (B_PROJ, F32), (3 * W_MIX, F32),
            (2 * W_MIX, F32), (W_MIX, BF16), (W_MIX, BF16), (W_MIX, BF16), (IDX_HEADS * LANES, BF16), (LANES, BF16)]
    return pl.pallas_call(
        _inproj_kernel,
        grid=(n // tm,),
        in_specs=[row(dm), full(lw['wa']), full(lw['wih']), full(lw['wil']), full(lw['wb']), full(lw['wc']),
                  full(lw['wd']), tab(), tab(), tab(), full(lw['idx_ln']), full(pq), full(pk)],
        out_specs=[row(w) for w, _ in outs],
        out_shape=[jax.ShapeDtypeStruct((n, w), dt) for w, dt in outs],
        compiler_params=_cparams(("parallel",)),
        name="inproj",
    )(x2d, lw['wa'], lw['wih'], lw['wil'], lw['wb'], lw['wc'], lw['wd'], tabs[0], tabs[1], tabs[2], lw['idx_ln'],
      pq, pk)


def _float_order_key(score):
    score = jnp.where(score == 0.0, 0.0, score)
    bits = pltpu.bitcast(score, I32)
    return jnp.where(bits < 0, bits ^ jnp.int32(0x7FFFFFFF), bits)


def _kth_largest_key(count_ge, topk, shape, two_bits=False):
    kf = jnp.float32(topk)
    thr0 = jnp.where(count_ge(jnp.zeros(shape, I32)) >= kf, jnp.int32(0), jnp.int32(INT_MIN))
    if not two_bits:
        def body(i, thr):
            cand = thr + jnp.left_shift(jnp.int32(1), jnp.int32(30) - i)
            return jnp.where(count_ge(cand) >= kf, cand, thr)

        return lax.fori_loop(0, 31, body, thr0)

    cand = thr0 + jnp.int32(1 << 30)
    thr1 = jnp.where(count_ge(cand) >= kf, cand, thr0)

    def body2(i, thr):
        unit = jnp.left_shift(jnp.int32(1), jnp.int32(28) - 2 * i)
        steps = jnp.zeros(shape, I32)
        for mult in (1, 2, 3):
            steps = steps + jnp.where(count_ge(thr + mult * unit) >= kf, 1, 0)
        return thr + steps * unit

    return lax.fori_loop(0, 15, body2, thr1)


def _upper_tri_ones(n):
    return jnp.where(_iota((n, n), 0) <= _iota((n, n), 1), 1.0, 0.0).astype(BF16)


def _tie_select(key, thr, adm, need, off, tri):
    eq = jnp.where(key == thr, jnp.where(adm, 1.0, 0.0), 0.0)
    pre = _dg(eq.astype(BF16), tri, NN)
    take_tie = jnp.where((pre + off) <= need, eq, 0.0)
    sel = jnp.where(key > thr, 1.0, take_tie)
    return sel, off + pre[:, pre.shape[1] - 1:pre.shape[1]]


ATT_KC = 512


def _attn_prompt_kernel(topk, qat_ref, qi3_ref, kiwq_ref, k_ref, v_ref, ki3_ref, o_ref, key_scr, lg_scr):
    qb = qat_ref.shape[0]
    kc = key_scr.shape[1]
    j = pl.program_id(1)
    n_chunks = (j * qb + qb + kc - 1) // kc
    krow = _iota((kc, qb), 0)
    qpos = j * qb + _iota((kc, qb), 1)
    qst = jnp.concatenate([qi3_ref[:, h * LANES:(h + 1) * LANES] for h in range(IDX_HEADS)], axis=0)
    kiw_t = kiwq_ref[...].T
    wi = [kiw_t[IDX_DIM + h:IDX_DIM + h + 1, :] * (IDX_DIM ** -0.5) for h in range(IDX_HEADS)]

    def score_chunk(c, carry):
        base = pl.multiple_of(c * kc, kc)
        dots = _dg(ki3_ref[pl.ds(base, kc), :], qst, NT)
        score = wi[0] * jnp.maximum(dots[:, 0:qb], 0.0)
        for h in range(1, IDX_HEADS):
            score = score + wi[h] * jnp.maximum(dots[:, h * qb:(h + 1) * qb], 0.0)
        key_scr[c] = jnp.where(base + krow <= qpos, _float_order_key(score), jnp.int32(INT_MIN))
        return carry

    lax.fori_loop(0, n_chunks, score_chunk, 0)

    acc_rows = 4 * SUBLANES

    def count(pred):
        def body(c, acc):
            hit = jnp.where(pred(key_scr[c]), 1.0, 0.0)
            return acc + jnp.sum(hit.reshape(kc // acc_rows, acc_rows, qb), axis=0)

        acc = lax.fori_loop(0, n_chunks, body, jnp.zeros((acc_rows, qb), F32))
        return jnp.sum(acc, axis=0, keepdims=True)

    thr = _kth_largest_key(lambda cand: count(lambda k: k >= cand), topk, (1, qb))
    need = jnp.float32(topk) - count(lambda k: k > thr)

    lane_q = _iota((1, A_HEADS * A_HEAD_DIM), 1)
    q = qat_ref[...]
    qm = jnp.concatenate([jnp.where(lane_q // A_HEAD_DIM == h, q, jnp.zeros_like(q)) for h in range(A_HEADS)],
                         axis=0)
    tril = jnp.where(_iota((kc, kc), 0) >= _iota((kc, kc), 1), 1.0, 0.0).astype(BF16)
    rows4 = A_HEADS * qb
    lane_blocks = lambda a: [a[:, i * LANES:(i + 1) * LANES] for i in range(kc // LANES)]

    def logits_chunk(c, carry):
        off, m = carry
        base = pl.multiple_of(c * kc, kc)
        key = key_scr[c]
        eq = jnp.where(key == thr, jnp.where(base + krow <= qpos, 1.0, 0.0), 0.0)
        pre = _dg(tril, eq.astype(BF16), NN)
        take_tie = jnp.where((pre + off) <= need, eq, 0.0)
        sel = jnp.where(key > thr, 1.0, take_tie)
        bias = jnp.where(sel > 0.5, 0.0, NEG_BIG).T
        lg = _dg(qm, k_ref[pl.ds(base, kc), :], NT) + jnp.concatenate([bias] * A_HEADS, axis=0)
        lg_scr[c] = lg
        return off + pre[kc - 1:kc, :], functools.reduce(jnp.maximum, lane_blocks(lg), m)

    _, m = lax.fori_loop(0, n_chunks, logits_chunk,
                         (jnp.zeros((1, qb), F32), jnp.full((rows4, LANES), NEG_BIG, F32)))
    m = jnp.max(m, axis=1, keepdims=True)

    def value_chunk(c, carry):
        l, acc = carry
        base = pl.multiple_of(c * kc, kc)
        p = jnp.exp(lg_scr[c] - m)
        return (functools.reduce(jnp.add, lane_blocks(p), l),
                acc + _dg(p.astype(BF16), v_ref[pl.ds(base, kc), :], NN))

    l, acc = lax.fori_loop(0, n_chunks, value_chunk,
                           (jnp.zeros((rows4, LANES), F32), jnp.zeros((rows4, A_HEADS * A_HEAD_DIM), F32)))
    acc = acc / jnp.sum(l, axis=1, keepdims=True)
    out = jnp.zeros((qb, A_HEADS * A_HEAD_DIM), F32)
    for h in range(A_HEADS):
        out = out + jnp.where(lane_q // A_HEAD_DIM == h, acc[h * qb:(h + 1) * qb], 0.0)
    o_ref[...] = out


ATT_QB = 256


def _attn_prompt(qat, qi3, kiw, kbf, vbf, ki3, b, t):
    qb = min(ATT_QB, t)
    nq = t // qb
    kc = min(ATT_KC, t)
    topk = min(TOPK_MAX, t // 4)
    qspec = lambda w: pl.BlockSpec((qb, w), lambda bi, j: (bi * nq + j, 0))
    kspec = lambda w: pl.BlockSpec((t, w), lambda bi, j: (bi, 0))
    return pl.pallas_call(
        functools.partial(_attn_prompt_kernel, topk),
        grid=(b, nq),
        in_specs=[qspec(W_MIX), qspec(IDX_HEADS * LANES), qspec(LANES), kspec(W_MIX), kspec(W_MIX), kspec(LANES)],
        out_specs=qspec(W_MIX),
        out_shape=jax.ShapeDtypeStruct((b * t, W_MIX), F32),
        scratch_shapes=[pltpu.VMEM((t // kc, kc, qb), I32), pltpu.VMEM((t // kc, A_HEADS * qb, kc), F32)],
        compiler_params=_cparams(("parallel", "arbitrary")),
        name="attn_prompt",
    )(qat, qi3, kiw, kbf, vbf, ki3)


SAMPLE_GB = 4


def _attn_sample_kernel(topk, n_pages, t_new, layer, pt_ref, qm_ref, qst_ref, wst_ref, knew_ref, vnew_ref, kinew_ref,
                        ck_ref, cv_ref, cik_ref, o_ref, key_scr, bias_scr, lg_scr, k_pages, v_pages, ik_pages, sems):
    gb = SAMPLE_GB
    step = pl.program_id(0)
    slot = step % 2
    rows = SUBLANES
    n_chunks = n_pages + 1
    pad_rows = LANES - SUBLANES

    def page_copies(st, sl):
        copies = []
        for g in range(gb):
            for p in range(n_pages):
                page = pt_ref[(st * gb + g) * n_pages + p]
                i = g * n_pages + p
                copies.append(pltpu.make_async_copy(ck_ref.at[layer, page], k_pages.at[sl, i], sems.at[0, sl]))
                copies.append(pltpu.make_async_copy(cv_ref.at[layer, page], v_pages.at[sl, i], sems.at[1, sl]))
                copies.append(pltpu.make_async_copy(cik_ref.at[layer, page], ik_pages.at[sl, i], sems.at[2, sl]))
        return copies

    @pl.when(step == 0)
    def _():
        for cp in page_copies(0, 0):
            cp.start()

    @pl.when(step + 1 < pl.num_programs(0))
    def _():
        for cp in page_copies(step + 1, 1 - slot):
            cp.start()

    for cp in page_copies(step, slot):
        cp.wait()

    def chunk_t(pages, new_ref, g, c):
        if c < n_pages:
            return pages[slot, g * n_pages + c]
        new = new_ref[g]
        return jnp.concatenate([new, jnp.zeros((pad_rows, new.shape[1]), F32)], axis=0).T

    def adm_of_chunk(c, nrows):
        if c < n_pages:
            return jnp.full((nrows, LANES), True)
        t = _iota((nrows, LANES), 0) % rows
        return _iota((nrows, LANES), 1) <= jnp.minimum(t, t_new - 1)

    for g in range(gb):
        qsh, qsl = _split2(qst_ref[g])
        wst = wst_ref[g]
        for c in range(n_chunks):
            ki = chunk_t(ik_pages, kinew_ref, g, c)[0:IDX_DIM]
            kih, kil = _split2(ki)
            dots = _dg(qsh, kih, NN) + (_dg(qsh, kil, NN) + _dg(qsl, kih, NN))
            sc = wst * jnp.maximum(dots * (IDX_DIM ** -0.5), 0.0)
            score = (sc[0:rows] + sc[rows:2 * rows]) + (sc[2 * rows:3 * rows] + sc[3 * rows:4 * rows])
            key_scr[g * rows:(g + 1) * rows, c * LANES:(c + 1) * LANES] = jnp.where(
                adm_of_chunk(c, rows), _float_order_key(score), jnp.int32(INT_MIN))

    all_rows = gb * rows
    count_ge = lambda cand: jnp.sum(jnp.where(key_scr[...] >= cand, 1.0, 0.0), axis=1, keepdims=True)
    thr = _kth_largest_key(count_ge, topk, (all_rows, 1), two_bits=True)
    need = jnp.float32(topk) - jnp.sum(jnp.where(key_scr[...] > thr, 1.0, 0.0), axis=1, keepdims=True)
    tri = _upper_tri_ones(LANES)
    off = jnp.zeros((all_rows, 1), F32)
    for c in range(n_chunks):
        sel, off = _tie_select(key_scr[:, c * LANES:(c + 1) * LANES], thr, adm_of_chunk(c, all_rows), need, off, tri)
        bias_scr[:, c * LANES:(c + 1) * LANES] = jnp.where(sel > 0.5, 0.0, NEG_BIG)

    lane_q = _iota((1, A_HEADS * A_HEAD_DIM), 1)
    hrows = A_HEADS * rows
    for g in range(gb):
        qm = qm_ref[g].astype(BF16)
        for c in range(n_chunks):
            kt = chunk_t(k_pages, knew_ref, g, c).astype(BF16)
            bias = bias_scr[g * rows:(g + 1) * rows, c * LANES:(c + 1) * LANES]
            lg_scr[g * hrows:(g + 1) * hrows, c * LANES:(c + 1) * LANES] = (
                _dg(qm, kt, NN) * (A_HEAD_DIM ** -0.5) + jnp.concatenate([bias] * A_HEADS, axis=0))
    lg = lg_scr[...]
    m = jnp.max(lg, axis=1, keepdims=True)
    lg_scr[...] = jnp.exp(lg - m)
    den = jnp.sum(lg_scr[...], axis=1, keepdims=True)
    for g in range(gb):
        acc = jnp.zeros((hrows, A_HEADS * A_HEAD_DIM), F32)
        for c in range(n_chunks):
            vt = chunk_t(v_pages, vnew_ref, g, c).astype(BF16)
            acc = acc + _dg(lg_scr[g * hrows:(g + 1) * hrows, c * LANES:(c + 1) * LANES].astype(BF16), vt, NT)
        acc = acc / den[g * hrows:(g + 1) * hrows]
        out = jnp.zeros((rows, A_HEADS * A_HEAD_DIM), F32)
        for h in range(A_HEADS):
            out = out + jnp.where(lane_q // A_HEAD_DIM == h, acc[h * rows:(h + 1) * rows], 0.0)
        o_ref[g] = out


def _attn_sample(layer, q, qi, kiw, k_new, v_new, page_table, cache_kt, cache_vt, cache_ikt):
    bd, ts, _ = q.shape
    gb = SAMPLE_GB
    n_pages = page_table.shape[1]
    rows = SUBLANES
    topk = min(TOPK_MAX, (n_pages * PAGE_SIZE + ts) // 4)
    padt = lambda a: jnp.pad(a, ((0, 0), (0, rows - ts), (0, 0)))
    qp = padt(q)
    head_of_lane = jnp.arange(A_HEADS * A_HEAD_DIM) // A_HEAD_DIM
    qm = jnp.where(head_of_lane[None, None, None, :] == jnp.arange(A_HEADS)[None, :, None, None],
                   qp[:, None, :, :], 0.0).reshape(bd, A_HEADS * rows, A_HEADS * A_HEAD_DIM)
    qst = padt(qi).reshape(bd, rows, IDX_HEADS, IDX_DIM).transpose(0, 2, 1, 3).reshape(bd, IDX_HEADS * rows, IDX_DIM)
    wi = padt(kiw[:, :, IDX_DIM:IDX_DIM + IDX_HEADS])
    wst = jnp.broadcast_to(wi.transpose(0, 2, 1).reshape(bd, IDX_HEADS * rows, 1), (bd, IDX_HEADS * rows, LANES))
    kp, vp, kinew = padt(k_new), padt(v_new), padt(kiw)
    per_b = lambda r, w: pl.BlockSpec((gb, r, w), lambda bi, pt: (bi, 0, 0))

    in_specs = [per_b(A_HEADS * rows, W_MIX), per_b(IDX_HEADS * rows, IDX_DIM), per_b(IDX_HEADS * rows, LANES),
                per_b(rows, W_MIX), per_b(rows, W_MIX), per_b(rows, LANES)]
    in_specs += [pl.BlockSpec(memory_space=pl.ANY)] * 3
    width = (n_pages + 1) * LANES
    n_bufs = gb * n_pages
    out = pl.pallas_call(
        functools.partial(_attn_sample_kernel, topk, n_pages, ts, layer),
        grid_spec=pltpu.PrefetchScalarGridSpec(
            num_scalar_prefetch=1,
            grid=(bd // gb,),
            in_specs=in_specs,
            out_specs=pl.BlockSpec((gb, rows, W_MIX), lambda bi, pt: (bi, 0, 0)),
            scratch_shapes=[pltpu.VMEM((gb * rows, width), I32), pltpu.VMEM((gb * rows, width), F32),
                            pltpu.VMEM((gb * A_HEADS * rows, width), F32),
                            pltpu.VMEM((2, n_bufs, W_MIX, PAGE_SIZE), F32),
                            pltpu.VMEM((2, n_bufs, W_MIX, PAGE_SIZE), F32),
                            pltpu.VMEM((2, n_bufs, IDX_DIM, PAGE_SIZE), F32),
                            pltpu.SemaphoreType.DMA((3, 2))]),
        out_shape=jax.ShapeDtypeStruct((bd, rows, W_MIX), F32),
        compiler_params=_cparams(("arbitrary",)),
        name="attn_sample",
    )(page_table.reshape(-1), qm, qst, wst, kp, vp, kinew, cache_kt, cache_vt, cache_ikt)
    return out[:, :ts]


def _rwkv_features(p, prev, mu, w6, vecs):
    z = p + (prev - p) * mu
    r = z[:, 0:W_MIX]
    k = z[:, W_MIX:2 * W_MIX]
    v = z[:, 2 * W_MIX:3 * W_MIX]
    x6 = z[:, 3 * W_MIX:B_PROJ]
    lane = _iota(x6.shape, 1)
    act = jnp.where(lane < DECAY_LORA, jnp.tanh(x6),
                    jnp.where(lane < DECAY_LORA + AAA_LORA, x6, _sigmoid(x6)))
    lo = _mm(act, w6)
    w0, a0, kkw, ka = vecs[0:1], vecs[1:2], vecs[2:3], vecs[3:4]
    w_log = -_softplus(-(w0 + lo[:, 0:W_MIX])) - 0.5
    asig = _sigmoid(a0 + lo[:, W_MIX:2 * W_MIX])
    kk = k * kkw
    ss = _mm_xl(kk * kk, _head_block_ones(W_MIX, B_HEAD))
    kk = kk / jnp.maximum(jnp.sqrt(ss), 1e-12)
    return (r, -jnp.exp(w_log), k * (1.0 + (asig - 1.0) * ka), v, -kk, kk * asig, lo[:, 2 * W_MIX:3 * W_MIX])


def _rwkv_prep_sample_kernel(bd, p_ref, shift_ref, mu_ref, w6_ref, vec_ref, *outs):
    p = p_ref[...]
    prev = jnp.concatenate([shift_ref[...], p[0:p.shape[0] - bd]], axis=0)
    for ref, val in zip(outs, _rwkv_features(p, prev, mu_ref[...], w6_ref[...], vec_ref[...])):
        ref[...] = val


def _rwkv_prep_sample(pb, shift, lw, bd):
    n = pb.shape[0]
    full = lambda a: pl.BlockSpec(a.shape, lambda i: (0, 0))
    return pl.pallas_call(
        functools.partial(_rwkv_prep_sample_kernel, bd),
        grid=(1,),
        in_specs=[full(pb), full(shift), full(lw['rwkv_mu']), full(lw['w6']), full(lw['rwkv_vec'])],
        out_specs=[pl.BlockSpec((n, W_MIX), lambda i: (0, 0))] * 7,
        out_shape=[jax.ShapeDtypeStruct((n, W_MIX), F32)] * 7,
        compiler_params=_cparams(("arbitrary",)),
        name="rwkv_prep_sample",
    )(pb, shift, lw['rwkv_mu'], lw['w6'], lw['rwkv_vec'])


def _rwkv_post(y, r, k, v, g, vecs, jhead):
    lnx_g, lnx_b, rk = vecs[4:5], vecs[5:6], vecs[6:7]
    mu = _mm_xl(y, jhead) * (1.0 / B_HEAD)
    d = y - mu
    var = _mm_xl(d * d, jhead) * (1.0 / B_HEAD)
    yn = d * lax.rsqrt(var + LNX_EPS) * lnx_g + lnx_b
    bonus = _mm_xl(r * k * rk, jhead) * v
    return (yn + bonus) * g


def _rwkv_chunk_kernel(gb, p_ref, mu_ref, w6_ref, vec_ref, y_ref, sfin_ref, s_scr, p_scr):
    c = p_ref.shape[1]
    w = W_MIX
    ci = pl.program_id(1)
    pad = SUBLANES

    @pl.when(ci == 0)
    def _():
        s_scr[...] = jnp.zeros(s_scr.shape, F32)
        p_scr[:, pad - 1:pad, :] = jnp.zeros((gb, 1, B_PROJ), F32)

    row_c = _iota((c, c), 0)
    col_c = _iota((c, c), 1)
    incl = row_c >= col_c
    strict = row_c > col_c
    ltri = jnp.where(incl, 1.0, 0.0).astype(BF16)
    eye_c = jnp.where(row_c == col_c, 1.0, 0.0).astype(F32)
    jhead = _head_block_ones(w, B_HEAD)
    bm = jhead > 0.5
    eye_w = _iota((w, w), 0) == _iota((w, w), 1)
    lane = _iota((1, w), 1)
    vecs = vec_ref[...]
    levels = int(math.log2(c))

    masks = [lane // B_HEAD == h for h in range(B_HEADS)]
    pairs = [(bi, h) for bi in range(gb) for h in range(B_HEADS)]
    seq = []
    for bi in range(gb):
        p = p_ref[bi]
        p_scr[bi, pad:pad + c, :] = p
        prev = p_scr[bi, pad - 1:pad - 1 + c, :]
        r, lw, k, v, a, b, g = _rwkv_features(p, prev, mu_ref[...], w6_ref[...], vecs)
        p_scr[bi, pad - 1:pad, :] = p[c - 1:c, :]
        cum = _mm_xr(ltri, lw)
        last = cum[c - 1:c, :]
        p_inv = jnp.exp(-cum)
        p_end = jnp.exp(last - cum)
        seq.append(dict(r=r, k=k, v=v, g=g, last=last, at=a * jnp.exp(cum - lw), rt=r * jnp.exp(cum),
                        bt=(b * p_inv).astype(BF16), kt=(k * p_inv).astype(BF16),
                        bpt=(b * p_end).T.astype(BF16), kpt=(k * p_end).T.astype(BF16), vb=v.astype(BF16)))
    n_mat, aak, brb, brk = {}, {}, {}, {}
    for bi, h in pairs:
        s = seq[bi]
        ar = jnp.concatenate([jnp.where(masks[h], s['at'], 0.0), jnp.where(masks[h], s['rt'], 0.0)],
                             axis=0).astype(BF16)
        mb = _dg(ar, s['bt'], NT)
        mk = _dg(ar, s['kt'], NT)
        n_mat[bi, h] = jnp.where(strict, mb[0:c], 0.0)
        aak[bi, h] = jnp.where(strict, mk[0:c], 0.0).astype(BF16)
        brb[bi, h] = jnp.where(incl, mb[c:2 * c], 0.0).astype(BF16)
        brk[bi, h] = jnp.where(incl, mk[c:2 * c], 0.0).astype(BF16)
    tinv = {p: eye_c + n_mat[p] for p in pairs}
    npow = n_mat
    for _ in range(levels - 1):
        npow = {p: _mm(npow[p], npow[p]) for p in pairs}
        tinv = {p: tinv[p] + _mm(tinv[p], npow[p]) for p in pairs}
    tinv = {p: tinv[p].astype(BF16) for p in pairs}
    x = {(bi, h): _dg(aak[bi, h], seq[bi]['vb'], NN).astype(BF16) for bi, h in pairs}
    for bi in range(gb):
        s = seq[bi]
        atb = s['at'].astype(BF16)
        wt = jnp.zeros((c, w), F32)
        u0 = jnp.zeros((c, w), F32)
        for h in range(B_HEADS):
            wt = wt + jnp.where(masks[h], _dg(tinv[bi, h], atb, NN), 0.0)
            u0 = u0 + jnp.where(masks[h], _dg(tinv[bi, h], x[bi, h], NN), 0.0)
        s['wtb'] = wt.astype(BF16)
        s['u0b'] = u0.astype(BF16)
    for bi in range(gb):
        s = seq[bi]
        ry = s['rt']
        y0 = jnp.zeros((c, w), F32)
        for h in range(B_HEADS):
            ry = ry + jnp.where(masks[h], _dg(brb[bi, h], s['wtb'], NN), 0.0)
            y0 = y0 + jnp.where(masks[h], _dg(brb[bi, h], s['u0b'], NN) + _dg(brk[bi, h], s['vb'], NN), 0.0)
        gt = jnp.where(eye_w, jnp.exp(s['last']), 0.0) + jnp.where(bm, _dg(s['bpt'], s['wtb'], NN), 0.0)
        ht = jnp.where(bm, _dg(s['bpt'], s['u0b'], NN) + _dg(s['kpt'], s['vb'], NN), 0.0)
        s0 = s_scr[bi]
        y = _mm3(ry, s0) + y0
        s_scr[bi] = _mm3(gt, s0) + ht
        y_ref[bi] = _rwkv_post(y, s['r'], s['k'], s['v'], s['g'], vecs, jhead)

    @pl.when(ci == pl.num_programs(1) - 1)
    def _():
        fold = jnp.where(_iota((w, B_HEAD), 0) % B_HEAD == _iota((w, B_HEAD), 1), 1.0, 0.0)
        for bi in range(gb):
            sfin_ref[bi] = _mm_xl(s_scr[bi], fold)


def _rwkv_scan_prompt(pb, lw, b, t, gb):
    c = RWKV_CHUNK
    full = lambda a: pl.BlockSpec(a.shape, lambda i, j: (0, 0))
    blk = lambda w: pl.BlockSpec((gb, c, w), lambda i, j: (i, j, 0))
    y, sfin = pl.pallas_call(
        functools.partial(_rwkv_chunk_kernel, gb),
        grid=(b // gb, t // c),
        in_specs=[blk(B_PROJ), full(lw['rwkv_mu']), full(lw['w6']), full(lw['rwkv_vec'])],
        out_specs=[blk(W_MIX), pl.BlockSpec((gb, W_MIX, B_HEAD), lambda i, j: (i, 0, 0))],
        out_shape=[jax.ShapeDtypeStruct((b, t, W_MIX), F32), jax.ShapeDtypeStruct((b, W_MIX, B_HEAD), F32)],
        scratch_shapes=[pltpu.VMEM((gb, W_MIX, W_MIX), F32), pltpu.VMEM((gb, c + SUBLANES, B_PROJ), F32)],
        compiler_params=_cparams(("parallel", "arbitrary")),
        name="rwkv_chunk",
    )(pb.reshape(b, t, B_PROJ), lw['rwkv_mu'], lw['w6'], lw['rwkv_vec'])
    state = sfin.reshape(b, B_HEADS, B_HEAD, B_HEAD).transpose(0, 1, 3, 2)
    return y.reshape(b * t, W_MIX), state


def _rwkv_step_kernel(gb, r_ref, lw_ref, k_ref, v_ref, a_ref, b_ref, g_ref, s_ref, vec_ref, y_ref, so_ref):
    ts = r_ref.shape[0]
    w = W_MIX
    jhead = _head_block_ones(w, B_HEAD)
    q1 = jnp.where(_iota((B_HEAD, w), 0) == _iota((B_HEAD, w), 1) % B_HEAD, 1.0, 0.0)
    q1g = jnp.concatenate([q1] * gb, axis=0)
    vecs = vec_ref[...]

    def rows(x):
        return jnp.concatenate([jnp.broadcast_to(x[bi:bi + 1, :], (B_HEAD, w)) for bi in range(gb)], axis=0)

    s = s_ref[...].reshape(gb * B_HEAD, w)
    for t in range(ts):
        r_t, k_t, v_t = r_ref[t], k_ref[t], v_ref[t]
        sa = _mm_xl(s * rows(a_ref[t]), jhead)
        vcol = _mm_xl(q1g * rows(v_t), jhead)
        s = s * rows(jnp.exp(lw_ref[t])) + sa * rows(b_ref[t]) + vcol * rows(k_t)
        yb = _mm_xl(s * rows(r_t), jhead)
        y_t = jnp.sum((yb * q1g).reshape(gb, B_HEAD, w), axis=1)
        y_ref[t] = _rwkv_post(y_t, r_t, k_t, v_t, g_ref[t], vecs, jhead)
    so_ref[...] = s.reshape(gb, B_HEAD, w)


def _rwkv_scan_sample(arrs, state, lw, bd, ts, gb):
    a3 = [x.reshape(ts, bd, W_MIX) for x in arrs]
    s_in = state.transpose(0, 2, 1, 3).reshape(bd, B_HEAD, W_MIX)
    blk = pl.BlockSpec((ts, gb, W_MIX), lambda i: (0, i, 0))
    sblk = pl.BlockSpec((gb, B_HEAD, W_MIX), lambda i: (i, 0, 0))
    y, s_out = pl.pallas_call(
        functools.partial(_rwkv_step_kernel, gb),
        grid=(bd // gb,),
        in_specs=[blk] * 7 + [sblk, pl.BlockSpec(lw['rwkv_vec'].shape, lambda i: (0, 0))],
        out_specs=[blk, sblk],
        out_shape=[jax.ShapeDtypeStruct((ts, bd, W_MIX), F32), jax.ShapeDtypeStruct((bd, B_HEAD, W_MIX), F32)],
        compiler_params=_cparams(("parallel",)),
        name="rwkv_step",
    )(*a3, s_in, lw['rwkv_vec'])
    new_state = s_out.reshape(bd, B_HEAD, B_HEADS, B_HEAD).transpose(0, 2, 1, 3)
    return y.reshape(ts * bd, W_MIX), new_state


def _lru_inputs(xc, gate_w_ref, vec):
    gates = _mm(xc, gate_w_ref[...])
    gate_x = _sigmoid(gates[:, 0:W_MIX] + vec[0:1])
    gate_a = _sigmoid(gates[:, W_MIX:2 * W_MIX] + vec[1:2])
    log_a = -LRU_C * gate_a * _softplus(-vec[2:3])
    a2 = jnp.exp(2.0 * log_a)
    neg_expm1 = -jnp.tanh(log_a) * (a2 + 1.0)
    u = jnp.sqrt(neg_expm1) * gate_x * xc
    return jnp.exp(log_a), u


def _cd_prompt_kernel(pc_ref, pd_ref, cw_ref, dw_ref, gw_ref, vec_ref,
                      yc_ref, yd_ref, bufc_ref, bufd_ref, h_ref, gx_scr, xb_scr, a_scr, u_scr, hs_scr, h_scr):
    tt = pc_ref.shape[0]
    ti = pl.program_id(1)
    pad = SUBLANES

    @pl.when(ti == 0)
    def _():
        gx_scr[0:pad, :] = jnp.zeros((pad, W_MIX), F32)
        xb_scr[0:pad, :] = jnp.zeros((pad, W_MIX), F32)
        h_scr[...] = jnp.zeros(h_scr.shape, F32)

    @pl.when(ti > 0)
    def _():
        gx_scr[0:pad, :] = gx_scr[tt:tt + pad, :]
        xb_scr[0:pad, :] = xb_scr[tt:tt + pad, :]

    vec = vec_ref[...]
    gx_scr[pad:pad + tt, :] = pc_ref[:, W_MIX:2 * W_MIX] * pc_ref[:, 2 * W_MIX:3 * W_MIX]
    conv = cw_ref[0:1, :] * gx_scr[pad - 2:pad - 2 + tt, :]
    for j in range(1, C_CONV):
        conv = conv + cw_ref[j:j + 1, :] * gx_scr[pad - 2 + j:pad - 2 + j + tt, :]
    yc_ref[...] = pc_ref[:, 0:W_MIX] * conv
    bufc_ref[...] = gx_scr[pad + tt - (C_CONV - 1):pad + tt, :]
    xb_scr[pad:pad + tt, :] = pd_ref[:, 0:W_MIX]
    xc = dw_ref[0:1, :] * xb_scr[pad - 3:pad - 3 + tt, :]
    for j in range(1, D_CONV):
        xc = xc + dw_ref[j:j + 1, :] * xb_scr[pad - 3 + j:pad - 3 + j + tt, :]
    xc = xc + vec[3:4]
    bufd_ref[...] = xb_scr[pad + tt - (D_CONV - 1):pad + tt, :]
    a, u = _lru_inputs(xc, gw_ref, vec)
    a_scr[...] = a
    u_scr[...] = u
    row8 = _iota((SUBLANES, W_MIX), 0)

    def group(gi, h):
        base = pl.multiple_of(gi * SUBLANES, SUBLANES)
        a8 = a_scr[pl.ds(base, SUBLANES), :]
        u8 = u_scr[pl.ds(base, SUBLANES), :]
        for d in (1, 2, 4):
            a_up = jnp.where(row8 >= d, pltpu.roll(a8, d, 0), 1.0)
            u_up = jnp.where(row8 >= d, pltpu.roll(u8, d, 0), 0.0)
            u8 = u8 + a8 * u_up
            a8 = a8 * a_up
        hs8 = a8 * h + u8
        hs_scr[pl.ds(base, SUBLANES), :] = hs8
        return hs8[SUBLANES - 1:SUBLANES, :]

    h = lax.fori_loop(0, tt // SUBLANES, group, h_scr[...], unroll=4)
    h_scr[...] = h
    h_ref[...] = h
    yd_ref[...] = hs_scr[...] * _gelu_tanh(pd_ref[:, W_MIX:2 * W_MIX])


def _cd_prompt(pc, pd, lw, b, t, tt):
    nt = t // tt
    row = lambda w: pl.BlockSpec((tt, w), lambda bi, ti: (bi * nt + ti, 0))
    full = lambda a: pl.BlockSpec(a.shape, lambda bi, ti: (0, 0))
    per_b = lambda r: pl.BlockSpec((None, r, W_MIX), lambda bi, ti: (bi, 0, 0))
    big = lambda: pltpu.VMEM((tt + 2 * SUBLANES, W_MIX), F32)
    tile = lambda: pltpu.VMEM((tt, W_MIX), F32)
    return pl.pallas_call(
        _cd_prompt_kernel,
        grid=(b, nt),
        in_specs=[row(3 * W_MIX), row(2 * W_MIX), full(lw['convc_w']), full(lw['convd_w']), full(lw['lru_gw']),
                  full(lw['lru_vec'])],
        out_specs=[row(W_MIX), row(W_MIX), per_b(C_CONV - 1), per_b(D_CONV - 1), per_b(1)],
        out_shape=[jax.ShapeDtypeStruct((b * t, W_MIX), F32), jax.ShapeDtypeStruct((b * t, W_MIX), F32),
                   jax.ShapeDtypeStruct((b, C_CONV - 1, W_MIX), F32),
                   jax.ShapeDtypeStruct((b, D_CONV - 1, W_MIX), F32), jax.ShapeDtypeStruct((b, 1, W_MIX), F32)],
        scratch_shapes=[big(), big(), tile(), tile(), tile(), pltpu.VMEM((1, W_MIX), F32)],
        compiler_params=_cparams(("parallel", "arbitrary")),
        name="cd_prompt",
    )(pc, pd, lw['convc_w'], lw['convd_w'], lw['lru_gw'], lw['lru_vec'])


def _cd_sample_kernel(pc_ref, pd_ref, bc_ref, bd_ref, h0_ref, cw_ref, dw_ref, gw_ref, vec_ref,
                      yc_ref, yd_ref, bufc_ref, bufd_ref, h_ref):
    ts = pc_ref.shape[0]
    vec = vec_ref[...]
    gx = [bc_ref[j] for j in range(C_CONV - 1)]
    gx += [pc_ref[t][:, W_MIX:2 * W_MIX] * pc_ref[t][:, 2 * W_MIX:3 * W_MIX] for t in range(ts)]
    for t in range(ts):
        conv = cw_ref[0:1, :] * gx[t]
        for j in range(1, C_CONV):
            conv = conv + cw_ref[j:j + 1, :] * gx[t + j]
        yc_ref[t] = pc_ref[t][:, 0:W_MIX] * conv
    for j in range(C_CONV - 1):
        bufc_ref[j] = gx[ts + j]
    xb = [bd_ref[j] for j in range(D_CONV - 1)] + [pd_ref[t][:, 0:W_MIX] for t in range(ts)]
    h = h0_ref[...]
    for t in range(ts):
        xc = dw_ref[0:1, :] * xb[t]
        for j in range(1, D_CONV):
            xc = xc + dw_ref[j:j + 1, :] * xb[t + j]
        xc = xc + vec[3:4]
        a, u = _lru_inputs(xc, gw_ref, vec)
        h = a * h + u
        yd_ref[t] = h * _gelu_tanh(pd_ref[t][:, W_MIX:2 * W_MIX])
    for j in range(D_CONV - 1):
        bufd_ref[j] = xb[ts + j]
    h_ref[...] = h


def _cd_sample(pc, pd, buf_c, buf_d, h0, lw, bd, ts):
    args = (pc, pd, buf_c, buf_d, h0, lw['convc_w'], lw['convd_w'], lw['lru_gw'], lw['lru_vec'])
    full = lambda a: pl.BlockSpec(a.shape, lambda i: (0,) * a.ndim)
    shapes = [(ts, bd, W_MIX), (ts, bd, W_MIX), (C_CONV - 1, bd, W_MIX), (D_CONV - 1, bd, W_MIX), (bd, W_MIX)]
    return pl.pallas_call(
        _cd_sample_kernel,
        grid=(1,),
        in_specs=[full(a) for a in args],
        out_specs=[pl.BlockSpec(s, lambda i, n=len(s): (0,) * n) for s in shapes],
        out_shape=[jax.ShapeDtypeStruct(s, F32) for s in shapes],
        compiler_params=_cparams(("arbitrary",)),
        name="cd_sample",
    )(*args)


def _outproj_kernel(x_ref, ya_ref, yb_ref, yc_ref, yd_ref, w_ref, ln_ref, o_ref):
    mix = _dg(ya_ref[...].astype(BF16), w_ref[0:W_MIX, :], NN)
    mix = mix + _dg(yb_ref[...].astype(BF16), w_ref[W_MIX:2 * W_MIX, :], NN)
    mix = mix + _dg(yc_ref[...].astype(BF16), w_ref[2 * W_MIX:3 * W_MIX, :], NN)
    mix = mix + _dg(yd_ref[...].astype(BF16), w_ref[3 * W_MIX:4 * W_MIX, :], NN)
    o_ref[...] = _layer_norm_rows(DEEPNORM_ALPHA * x_ref[...] + mix, ln_ref[0:1, :], ln_ref[1:2, :])


def _outproj(x2d, ys, lw, tm):
    n, dm = x2d.shape
    row = lambda w: pl.BlockSpec((tm, w), lambda i: (i, 0))
    full = lambda a: pl.BlockSpec(a.shape, lambda i: (0, 0))
    return pl.pallas_call(
        _outproj_kernel,
        grid=(n // tm,),
        in_specs=[row(dm)] + [row(W_MIX)] * 4 + [full(lw['w_out']), full(lw['ln1'])],
        out_specs=row(dm),
        out_shape=jax.ShapeDtypeStruct((n, dm), F32),
        compiler_params=_cparams(("parallel",)),
        name="outproj",
    )(x2d, *ys, lw['w_out'], lw['ln1'])


def _route(x, wr_h, wr_l, bias_col):
    xh, xl = _split2(x)
    logits = _dg(xh, wr_h, NN) + (_dg(xh, wr_l, NN) + _dg(xl, wr_h, NN))
    st = _sigmoid(logits.T[0:N_EXPERTS, :])
    sel = st + bias_col
    rows = [sel[e:e + 1, :] for e in range(N_EXPERTS)]
    in_top2 = []
    for e in range(N_EXPERTS):
        g0 = (e // EXPERTS_PER_GROUP) * EXPERTS_PER_GROUP
        rank = jnp.zeros(rows[e].shape, F32)
        for o in range(g0, g0 + EXPERTS_PER_GROUP):
            if o == e:
                continue
            ahead = (rows[o] >= rows[e]) if o < e else (rows[o] > rows[e])
            rank = rank + jnp.where(ahead, 1.0, 0.0)
        in_top2.append(rank < float(2))
    gscore = []
    for g in range(N_GROUPS):
        acc = jnp.zeros(rows[0].shape, F32)
        for e in range(g * EXPERTS_PER_GROUP, (g + 1) * EXPERTS_PER_GROUP):
            acc = acc + jnp.where(in_top2[e], rows[e], 0.0)
        gscore.append(acc)
    best = gscore[0]
    gidx = jnp.zeros(best.shape, I32)
    for g in range(1, N_GROUPS):
        better = gscore[g] > best
        best = jnp.where(better, gscore[g], best)
        gidx = jnp.where(better, g, gidx)
    picked = [jnp.where(in_top2[e] & (gidx == e // EXPERTS_PER_GROUP), st[e:e + 1, :], 0.0)
              for e in range(N_EXPERTS)]
    total = picked[0]
    for e in range(1, N_EXPERTS):
        total = total + picked[e]
    return jnp.concatenate(picked, axis=0) / total


MOE_EXPERTS_PER_STEP = 4


def _moe_kernel(x_ref, wrh_ref, wrl_ref, rb_ref, wgu_ref, wd_ref, ln_ref, o_ref, xb_scr, gate_scr, acc_scr):
    step = pl.program_id(1)
    per = MOE_EXPERTS_PER_STEP

    @pl.when(step == 0)
    def _():
        x = x_ref[...]
        xb_scr[...] = x.astype(BF16)
        gates = _route(x, wrh_ref[...], wrl_ref[...], rb_ref[...])
        pad = jnp.zeros((LANES - N_EXPERTS, gates.shape[1]), F32)
        gate_scr[...] = jnp.concatenate([gates, pad], axis=0).T
        acc_scr[...] = jnp.zeros(acc_scr.shape, F32)

    lane = _iota(gate_scr.shape, 1)
    hidden = []
    for j in range(per):
        gu = _dg(xb_scr[...], wgu_ref[j], NN)
        hj = gu[:, 0:D_EXPERT]
        hj = hj * _sigmoid(hj) * gu[:, D_EXPERT:2 * D_EXPERT]
        ge = jnp.sum(jnp.where(lane == step * per + j, gate_scr[...], 0.0), axis=1, keepdims=True)
        hidden.append((hj * ge).astype(BF16))
    w_down = wd_ref[...].reshape(per * D_EXPERT, wd_ref.shape[2])
    acc_scr[...] += _dg(jnp.concatenate(hidden, axis=1), w_down, NN)

    @pl.when(step == pl.num_programs(1) - 1)
    def _():
        o_ref[...] = _layer_norm_rows(DEEPNORM_ALPHA * x_ref[...] + acc_scr[...], ln_ref[0:1, :], ln_ref[1:2, :])


def _moe(x2d, lw, shared, tm):
    n, dm = x2d.shape
    row = pl.BlockSpec((tm, dm), lambda i, e: (i, 0))
    full = lambda a: pl.BlockSpec(a.shape, lambda i, e: (0, 0))
    return pl.pallas_call(
        _moe_kernel,
        grid=(n // tm, N_EXPERTS // MOE_EXPERTS_PER_STEP),
        in_specs=[row, full(shared['wr_h']), full(shared['wr_l']), full(shared['router_bias']),
                  pl.BlockSpec((MOE_EXPERTS_PER_STEP, dm, 2 * D_EXPERT), lambda i, e: (e, 0, 0)),
                  pl.BlockSpec((MOE_EXPERTS_PER_STEP, D_EXPERT, dm), lambda i, e: (e, 0, 0)), full(lw['ln2'])],
        out_specs=row,
        out_shape=jax.ShapeDtypeStruct((n, dm), F32),
        scratch_shapes=[pltpu.VMEM((tm, dm), BF16), pltpu.VMEM((tm, LANES), F32), pltpu.VMEM((tm, dm), F32)],
        compiler_params=_cparams(("parallel", "arbitrary")),
        name="moe",
    )(x2d, shared['wr_h'], shared['wr_l'], shared['router_bias'], lw['moe_wgu'], lw['moe_wd'], lw['ln2'])


def _rope_tables(pos, head_dim, rot_dim):
    half = rot_dim // 2
    inv_freq = ROPE_THETA ** (-jnp.arange(half, dtype=F32) / half)
    ang = pos.astype(F32)[:, None] * inv_freq[None, :]
    cos, sin = jnp.cos(ang), jnp.sin(ang)
    lane = jnp.arange(LANES)
    within = lane % head_dim
    idx = within % half
    first = within < half
    second = (within >= half) & (within < rot_dim)
    c = jnp.where((first | second)[None, :], cos[:, idx], 1.0)
    sm = jnp.where(first[None, :], -sin[:, idx], 0.0)
    sp = jnp.where(second[None, :], sin[:, idx], 0.0)
    return jnp.concatenate([c, sm, sp], axis=1)


def _block_diag_heads(w):
    h, d, _ = w.shape
    eye = jnp.eye(h, dtype=w.dtype)
    return (eye[:, None, :, None] * w[:, :, None, :]).reshape(h * d, h * d)


def _row_pad(v, width=LANES):
    return jnp.pad(v, (0, width - v.shape[0]))[None, :]


def _prep_layer(l, p):
    w_in = p['w_in'][l]
    ia = 3 * W_MIX
    wi_cols = jnp.pad(w_in[:, ia:A_COLS], ((0, 0), (0, 2 * LANES - (A_COLS - ia))))
    wih = wi_cols.astype(BF16)
    wil = (wi_cols - wih.astype(F32)).astype(BF16)
    b0 = A_COLS
    c0 = b0 + B_PROJ
    d0 = c0 + 3 * W_MIX
    w6 = jnp.zeros((LANES, 3 * W_MIX), F32)
    w6 = w6.at[0:DECAY_LORA, 0:W_MIX].set(p['rwkv_w2'][l])
    w6 = w6.at[DECAY_LORA:DECAY_LORA + AAA_LORA, W_MIX:2 * W_MIX].set(p['rwkv_a2'][l])
    w6 = w6.at[DECAY_LORA + AAA_LORA:LANES, 2 * W_MIX:3 * W_MIX].set(p['rwkv_g2'][l])
    zeros = jnp.zeros((W_MIX,), F32)
    return {
        'wa': w_in[:, 0:ia].astype(BF16), 'wih': wih, 'wil': wil,
        'wb': w_in[:, b0:c0].astype(BF16), 'wc': w_in[:, c0:d0].astype(BF16), 'wd': w_in[:, d0:].astype(BF16),
        'idx_ln': jnp.concatenate([_row_pad(p['idx_ln_g'][l]), _row_pad(p['idx_ln_b'][l])], axis=0),
        'rwkv_mu': p['rwkv_mu'][l][None, :],
        'w6': w6.astype(BF16),
        'rwkv_vec': jnp.stack([p['rwkv_w0'][l], p['rwkv_a0'][l], p['rwkv_kk'][l], p['rwkv_ka'][l],
                               p['rwkv_lnx_g'][l], p['rwkv_lnx_b'][l], p['rwkv_rk'][l].reshape(-1), zeros]),
        'convc_w': p['convc_w'][l], 'convd_w': p['convd_w'][l],
        'lru_gw': jnp.concatenate([_block_diag_heads(p['lru_gx_w'][l]), _block_diag_heads(p['lru_ga_w'][l])],
                                  axis=1).astype(BF16),
        'lru_vec': jnp.stack([p['lru_gx_b'][l], p['lru_ga_b'][l], p['lru_lam'][l], p['convd_b'][l]]),
        'w_out': p['w_out'][l].astype(BF16),
        'ln1': jnp.stack([p['ln1_g'][l], p['ln1_b'][l]]),
        'ln2': jnp.stack([p['ln2_g'][l], p['ln2_b'][l]]),
        'moe_wgu': jnp.concatenate([p['moe_w_gate'][l], p['moe_w_up'][l]], axis=-1).astype(BF16),
        'moe_wd': p['moe_w_down'][l].astype(BF16),
    }


def _prompt_layer(x2d, lw, shared, tabs, b, t):
    _, k, v, _, kiw, pb, pc, pd, qat, kbf, vbf, qi3, ki3 = _inproj(x2d, lw, tabs, t, min(512, t))
    ya = _attn_prompt(qat, qi3, kiw, kbf, vbf, ki3, b, t)
    yb, rwkv_state = _rwkv_scan_prompt(pb, lw, b, t, min(4, b))
    yc, yd, bufc, bufd, h = _cd_prompt(pc, pd, lw, b, t, min(512, t))
    x1 = _outproj(x2d, (ya, yb, yc, yd), lw, 512)
    x2 = _moe(x1, lw, shared, min(1024, b * t))
    st = (k.reshape(b, t, A_HEADS, A_HEAD_DIM), v.reshape(b, t, A_HEADS, A_HEAD_DIM),
          kiw[:, 0:IDX_DIM].reshape(b, t, IDX_DIM), rwkv_state,
          pb.reshape(b, t, B_PROJ)[:, t - 1:t], bufc, bufd, h.reshape(b, W_MIX))
    return x2, st


def _sample_layer(l, x2d, lw, shared, tabs, bd, ts, caches, page_table, states):
    n = bd * ts
    rwkv_state, shift, conv_c, conv_d, lru_h = states
    q, k, v, qi, kiw, pb, pc, pd = _inproj(x2d, lw, tabs, n, 256)[0:8]
    bm = lambda a: a.reshape(ts, bd, a.shape[-1]).transpose(1, 0, 2)
    ya = _attn_sample(l, bm(q), bm(qi), bm(kiw), bm(k), bm(v), page_table, *caches)
    ya = ya.transpose(1, 0, 2).reshape(n, W_MIX)
    arrs = _rwkv_prep_sample(pb, shift.reshape(bd, B_PROJ), lw, bd)
    yb, new_rwkv = _rwkv_scan_sample(arrs, rwkv_state, lw, bd, ts, 16)
    yc, yd, bufc, bufd, h = _cd_sample(pc.reshape(ts, bd, 3 * W_MIX), pd.reshape(ts, bd, 2 * W_MIX),
                                       conv_c.transpose(1, 0, 2), conv_d.transpose(1, 0, 2), lru_h, lw, bd, ts)
    x1 = _outproj(x2d, (ya, yb, yc.reshape(n, W_MIX), yd.reshape(n, W_MIX)), lw, 256)
    x2 = _moe(x1, lw, shared, n)
    kb, vb, kib = bm(k), bm(v), bm(kiw)
    st = (kb.reshape(bd, ts, A_HEADS, A_HEAD_DIM), vb.reshape(bd, ts, A_HEADS, A_HEAD_DIM), kib[:, :, 0:IDX_DIM],
          new_rwkv, pb.reshape(ts, bd, B_PROJ)[ts - 1][:, None, :], bufc.transpose(1, 0, 2),
          bufd.transpose(1, 0, 2), h)
    return x2, st


def kernel(x_prompt, x_sample, cache_k, cache_v, cache_ik, page_table, state_rwkv, state_rwkv_shift, state_conv_c, state_conv_d, state_lru, w_in, w_out, idx_ln_g, idx_ln_b, rwkv_mu, rwkv_w0, rwkv_w2, rwkv_a0, rwkv_a2, rwkv_g2, rwkv_kk, rwkv_ka, rwkv_rk, rwkv_lnx_g, rwkv_lnx_b, convc_w, convd_w, convd_b, lru_gx_w, lru_gx_b, lru_ga_w, lru_ga_b, lru_lam, ln1_g, ln1_b, ln2_g, ln2_b, w_router, router_bias, moe_w_gate, moe_w_up, moe_w_down):
    p = dict(w_in=w_in, w_out=w_out, idx_ln_g=idx_ln_g, idx_ln_b=idx_ln_b, rwkv_mu=rwkv_mu, rwkv_w0=rwkv_w0,
             rwkv_w2=rwkv_w2, rwkv_a0=rwkv_a0, rwkv_a2=rwkv_a2, rwkv_g2=rwkv_g2, rwkv_kk=rwkv_kk, rwkv_ka=rwkv_ka,
             rwkv_rk=rwkv_rk, rwkv_lnx_g=rwkv_lnx_g, rwkv_lnx_b=rwkv_lnx_b, convc_w=convc_w, convd_w=convd_w,
             convd_b=convd_b, lru_gx_w=lru_gx_w, lru_gx_b=lru_gx_b, lru_ga_w=lru_ga_w, lru_ga_b=lru_ga_b,
             lru_lam=lru_lam, ln1_g=ln1_g, ln1_b=ln1_b, ln2_g=ln2_g, ln2_b=ln2_b, moe_w_gate=moe_w_gate,
             moe_w_up=moe_w_up, moe_w_down=moe_w_down)
    depth = w_in.shape[0]
    b, t, dm = x_prompt.shape
    bd, ts, _ = x_sample.shape
    past_len = page_table.shape[1] * PAGE_SIZE

    wr = jnp.pad(w_router, ((0, 0), (0, LANES - N_EXPERTS)))
    wr_h = wr.astype(BF16)
    shared = {'wr_h': wr_h, 'wr_l': (wr - wr_h.astype(F32)).astype(BF16), 'router_bias': router_bias[:, None]}

    pos_p = jnp.arange(t)
    pos_s = jnp.repeat(past_len + jnp.arange(ts), bd)
    tabs_p = (_rope_tables(pos_p, A_HEAD_DIM, ROT_DIM), _rope_tables(pos_p, IDX_DIM, IDX_ROT_DIM),
              _rope_tables(pos_p, LANES, IDX_ROT_DIM))
    tabs_s = (_rope_tables(pos_s, A_HEAD_DIM, ROT_DIM), _rope_tables(pos_s, IDX_DIM, IDX_ROT_DIM),
              _rope_tables(pos_s, LANES, IDX_ROT_DIM))

    n_phys = cache_k.shape[1]
    caches_t = (cache_k.transpose(0, 1, 3, 4, 2).reshape(depth, n_phys, W_MIX, PAGE_SIZE),
                cache_v.transpose(0, 1, 3, 4, 2).reshape(depth, n_phys, W_MIX, PAGE_SIZE),
                cache_ik.transpose(0, 1, 3, 2))
    xp = x_prompt.reshape(b * t, dm)
    xs = x_sample.transpose(1, 0, 2).reshape(ts * bd, dm)
    new_p = [[] for _ in range(8)]
    new_s = [[] for _ in range(8)]
    for l in range(depth):
        lw = _prep_layer(l, p)
        xp, st_p = _prompt_layer(xp, lw, shared, tabs_p, b, t)
        xs, st_s = _sample_layer(l, xs, lw, shared, tabs_s, bd, ts, caches_t, page_table,
                                 (state_rwkv[l], state_rwkv_shift[l], state_conv_c[l], state_conv_d[l],
                                  state_lru[l]))
        for i in range(8):
            new_p[i].append(st_p[i])
            new_s[i].append(st_s[i])
    y_p = xp.reshape(b, t, dm)
    y_s = xs.reshape(ts, bd, dm).transpose(1, 0, 2)
    return (y_p, y_s) + tuple(jnp.stack(a) for a in new_p) + tuple(jnp.stack(a) for a in new_s)
```
